```python
import math
import jax, jax.numpy as jnp
from jax import lax
import numpy as np

D_MODEL = 1024
BATCH = 2
SEQ = 8192
DEPTH = 2

MIX_WIDTH = D_MODEL
MLA_WIDTH = D_MODEL // 2
SGU_WIDTH = D_MODEL // 4
POOL_WIDTH = D_MODEL // 4

MLA_HEADS = 4
V_HEAD = MLA_WIDTH // MLA_HEADS
QK_NOPE = 64
QK_ROPE = 32
QK_HEAD = QK_NOPE + QK_ROPE
Q_LORA = D_MODEL // 4
KV_LORA = D_MODEL // 8
ROPE_THETA = 10000.0
Q_BLOCK = 128

SGU_HEADS = 4
SGU_HEAD_DIM = SGU_WIDTH // SGU_HEADS
CHUNK = 128

POOL_WINDOWS = (2, 4, 8, 16)
POOL_GROUPS = len(POOL_WINDOWS)
POOL_GROUP_DIM = POOL_WIDTH // POOL_GROUPS

IN_WIDTH = Q_LORA + KV_LORA + QK_ROPE + 2 * SGU_WIDTH + POOL_WIDTH
FFN_HIDDEN = -(-8 * D_MODEL // (3 * 256)) * 256
EPS = 1e-6

kernel_name = "hybrid_mla_sgu_pool_block"


def rms_norm(x, g):
    xf = x.astype(jnp.float32)
    y = xf * lax.rsqrt(jnp.mean(xf * xf, axis=-1, keepdims=True) + EPS)
    return (y * g.astype(jnp.float32)).astype(x.dtype)


def apply_rope(x, positions):
    half = x.shape[-1] // 2
    inv_freq = 1.0 / (ROPE_THETA ** (jnp.arange(half, dtype=jnp.float32) / half))
    ang = positions.astype(jnp.float32)[:, :, None, None] * inv_freq
    cos, sin = jnp.cos(ang), jnp.sin(ang)
    xf = x.astype(jnp.float32)
    x1, x2 = xf[..., :half], xf[..., half:]
    return jnp.concatenate([x1 * cos - x2 * sin, x2 * cos + x1 * sin], axis=-1).astype(x.dtype)


def mla_mixer(q_lat, kv_lat, k_rope, positions, g_q_lat, w_q_up, g_kv_lat, w_kv_up, g_q_head, g_k_head):
    B, S, _ = q_lat.shape
    q = (rms_norm(q_lat, g_q_lat) @ w_q_up).reshape(B, S, MLA_HEADS, QK_HEAD)
    kv = (rms_norm(kv_lat, g_kv_lat) @ w_kv_up).reshape(B, S, MLA_HEADS, QK_NOPE + V_HEAD)
    k_nope, v = kv[..., :QK_NOPE], kv[..., QK_NOPE:]
    k_pe = jnp.broadcast_to(k_rope[:, :, None, :], (B, S, MLA_HEADS, QK_ROPE))
    k = jnp.concatenate([k_nope, k_pe], axis=-1)
    q = rms_norm(q, g_q_head)
    k = rms_norm(k, g_k_head)
    q = jnp.concatenate([q[..., :QK_NOPE], apply_rope(q[..., QK_NOPE:], positions)], axis=-1)
    k = jnp.concatenate([k[..., :QK_NOPE], apply_rope(k[..., QK_NOPE:], positions)], axis=-1)

    n_blocks = S // Q_BLOCK
    scale = 1.0 / math.sqrt(QK_HEAD)
    qb = q.reshape(B, n_blocks, Q_BLOCK, MLA_HEADS, QK_HEAD).transpose(1, 0, 2, 3, 4)
    kpos = jnp.arange(S)

    def attend_block(args):
        qi, bi = args
        s = jnp.einsum('bqhd,bkhd->bhqk', qi, k).astype(jnp.float32) * scale
        qpos = bi * Q_BLOCK + jnp.arange(Q_BLOCK)
        causal = kpos[None, :] <= qpos[:, None]
        s = jnp.where(causal[None, None], s, jnp.finfo(jnp.float32).min)
        p = jax.nn.softmax(s, axis=-1)
        return jnp.einsum('bhqk,bkhd->bqhd', p.astype(v.dtype), v)

    o = lax.map(attend_block, (qb, jnp.arange(n_blocks)))
    return o.transpose(1, 0, 2, 3, 4).reshape(B, S, MLA_WIDTH)


def sgu_mixer(uv, g_v, w_spatial, b_spatial):
    B, S, _ = uv.shape
    u, v = uv[..., :SGU_WIDTH], uv[..., SGU_WIDTH:]
    v = rms_norm(v, g_v)
    vc = v.reshape(B, S // CHUNK, CHUNK, SGU_HEADS, SGU_HEAD_DIM)
    w = w_spatial * jnp.tril(jnp.ones((CHUNK, CHUNK), dtype=w_spatial.dtype))
    zc = jnp.einsum('hts,bcshd->bcthd', w, vc) + b_spatial.T[None, None, :, :, None]
    return u * zc.reshape(B, S, SGU_WIDTH)


def pool_mixer(p, w_pool, pool_scale):
    B, S, _ = p.shape
    pf = p.astype(jnp.float32).reshape(B, S, POOL_GROUPS, POOL_GROUP_DIM)
    t1 = jnp.arange(1, S + 1, dtype=jnp.float32)
    outs = []
    for g, win in enumerate(POOL_WINDOWS):
        xg = pf[:, :, g]
        cs = jnp.cumsum(xg, axis=1)
        cs_shift = jnp.pad(cs, ((0, 0), (win, 0), (0, 0)))[:, :S]
        count = jnp.minimum(t1, float(win))[None, :, None]
        outs.append((cs - cs_shift) / count - xg)
    m = jnp.stack(outs, axis=2).astype(p.dtype)
    y = jnp.einsum('bsgc,gcd->bsgd', m, w_pool).reshape(B, S, POOL_WIDTH)
    return y * pool_scale


def setup_inputs(seed: int = 0) -> dict:
    key = jax.random.key(seed)
    ks = jax.random.split(key, 24)
    f32 = jnp.float32

    def dense(k, shape, fan_in):
        return jax.random.normal(k, shape, f32) * fan_in ** -0.5

    def gain(k, shape):
        return 1.0 + 0.01 * jax.random.normal(k, shape, f32)

    x = jax.random.normal(ks[0], (BATCH, SEQ, D_MODEL), f32)
    start = jax.random.randint(ks[1], (BATCH, 1), 0, 4096, dtype=jnp.int32)
    positions = start + jnp.arange(SEQ, dtype=jnp.int32)[None, :]
    return {
        "x": x,
        "positions": positions,
        "g_mix_norm": gain(ks[2], (DEPTH, D_MODEL)),
        "w_in": dense(ks[3], (DEPTH, D_MODEL, IN_WIDTH), D_MODEL),
        "g_q_lat": gain(ks[4], (DEPTH, Q_LORA)),
        "w_q_up": dense(ks[5], (DEPTH, Q_LORA, MLA_HEADS * QK_HEAD), Q_LORA),
        "g_kv_lat": gain(ks[6], (DEPTH, KV_LORA)),
        "w_kv_up": dense(ks[7], (DEPTH, KV_LORA, MLA_HEADS * (QK_NOPE + V_HEAD)), KV_LORA),
        "g_q_head": gain(ks[8], (DEPTH, QK_HEAD)),
        "g_k_head": gain(ks[9], (DEPTH, QK_HEAD)),
        "g_sgu_v": gain(ks[10], (DEPTH, SGU_WIDTH)),
        "w_spatial": dense(ks[11], (DEPTH, SGU_HEADS, CHUNK, CHUNK), CHUNK),
        "b_spatial": 1.0 + 0.01 * jax.random.normal(ks[12], (DEPTH, SGU_HEADS, CHUNK), f32),
        "w_pool": dense(ks[13], (DEPTH, POOL_GROUPS, POOL_GROUP_DIM, POOL_GROUP_DIM), POOL_GROUP_DIM),
        "pool_scale": 1.0 + 0.1 * jax.random.normal(ks[14], (DEPTH, POOL_WIDTH), f32),
        "g_out_mla": gain(ks[15], (DEPTH, MLA_WIDTH)),
        "g_out_sgu": gain(ks[16], (DEPTH, SGU_WIDTH)),
        "g_out_pool": gain(ks[17], (DEPTH, POOL_WIDTH)),
        "w_out": dense(ks[18], (DEPTH, MIX_WIDTH, D_MODEL), MIX_WIDTH),
        "g_ffn_norm": gain(ks[19], (DEPTH, D_MODEL)),
        "w_gate": dense(ks[20], (DEPTH, D_MODEL, FFN_HIDDEN), D_MODEL),
        "w_up": dense(ks[21], (DEPTH, D_MODEL, FFN_HIDDEN), D_MODEL),
        "w_down": dense(ks[22], (DEPTH, FFN_HIDDEN, D_MODEL), FFN_HIDDEN),
    }


def reference(x, positions, g_mix_norm, w_in, g_q_lat, w_q_up, g_kv_lat, w_kv_up, g_q_head, g_k_head,
              g_sgu_v, w_spatial, b_spatial, w_pool, pool_scale, g_out_mla, g_out_sgu, g_out_pool,
              w_out, g_ffn_norm, w_gate, w_up, w_down):
    o1 = Q_LORA
    o2 = o1 + KV_LORA
    o3 = o2 + QK_ROPE
    o4 = o3 + 2 * SGU_WIDTH
    for l in range(DEPTH):
        h = rms_norm(x, g_mix_norm[l])
        z = h @ w_in[l]
        q_lat, kv_lat, k_rope = z[..., :o1], z[..., o1:o2], z[..., o2:o3]
        uv, pin = z[..., o3:o4], z[..., o4:]
        a = mla_mixer(q_lat, kv_lat, k_rope, positions, g_q_lat[l], w_q_up[l], g_kv_lat[l],
                      w_kv_up[l], g_q_head[l], g_k_head[l])
        gm = sgu_mixer(uv, g_sgu_v[l], w_spatial[l], b_spatial[l])
        po = pool_mixer(pin, w_pool[l], pool_scale[l])
        mix = jnp.concatenate([rms_norm(a, g_out_mla[l]), rms_norm(gm, g_out_sgu[l]),
                               rms_norm(po, g_out_pool[l])], axis=-1)
        x = x + mix @ w_out[l]
        h = rms_norm(x, g_ffn_norm[l])
        x = x + (jax.nn.silu(h @ w_gate[l]) * (h @ w_up[l])) @ w_down[l]
    return x
```

```python
import functools
import math

import jax
import jax.numpy as jnp
from jax import lax
from jax.experimental import pallas as pl
from jax.experimental.pallas import tpu as pltpu

D_MODEL = 1024
MLA_HEADS = 4
V_HEAD = 128
QK_NOPE = 64
QK_ROPE = 32
QK_HEAD = QK_NOPE + QK_ROPE
Q_LORA = 256
KV_LORA = 128
ROPE_THETA = 10000.0
SGU_WIDTH = 256
SGU_HEADS = 4
SGU_HEAD_DIM = 64
CHUNK = 128
POOL_WIDTH = 256
POOL_WINDOWS = (2, 4, 8, 16)
POOL_GROUP_DIM = 64
MAX_WIN = max(POOL_WINDOWS)
FFN_HIDDEN = 2816
EPS = 1e-6

LANES = 128
HEAD_PAD = LANES
IN_PAD = Q_LORA + KV_LORA + LANES + 2 * SGU_WIDTH + POOL_WIDTH
ROPE_HALF = QK_ROPE // 2
ROPE_COPY_END = QK_HEAD + ROPE_HALF
Q_SCALE = math.log2(math.e) / math.sqrt(QK_HEAD)
NEG_BIG = -1e30

TM = 512
TQ = 512
TK = 512
FFN_CHUNK = 256
N_FFN_CHUNKS = FFN_HIDDEN // FFN_CHUNK
VMEM_LIMIT = 56 * 1024 * 1024

F32 = jnp.float32
BF16 = jnp.bfloat16


def _rms(v, g):
    return v * lax.rsqrt(jnp.mean(v * v, axis=-1, keepdims=True) + EPS) * g


def _rope_table_kernel(pos_ref, invf_ref, c_ref, s_ref):
    ang = pos_ref[...] * invf_ref[...]
    lane = lax.broadcasted_iota(jnp.int32, ang.shape, 1)
    cosv = jnp.cos(ang)
    sinv = jnp.sin(ang)
    c_ref[...] = jnp.where(lane < QK_NOPE, 1.0, jnp.where(lane < QK_HEAD, cosv, 0.0))
    first = (lane >= QK_NOPE) & (lane < QK_NOPE + ROPE_HALF)
    second = (lane >= QK_NOPE + ROPE_HALF) & (lane < QK_HEAD)
    s_ref[...] = jnp.where(first, -sinv, jnp.where(second, sinv, 0.0))


def _rope_tables(pos_f, invf_lane):
    n = pos_f.shape[0]
    return pl.pallas_call(
        _rope_table_kernel,
        grid=(n // TM,),
        in_specs=[pl.BlockSpec((TM, 1), lambda i: (i, 0)),
                  pl.BlockSpec((1, LANES), lambda i: (0, 0))],
        out_specs=[pl.BlockSpec((TM, LANES), lambda i: (i, 0)),
                   pl.BlockSpec((TM, LANES), lambda i: (i, 0))],
        out_shape=[jax.ShapeDtypeStruct((n, LANES), F32)] * 2,
        compiler_params=pltpu.CompilerParams(dimension_semantics=("parallel",)),
        name="rope_tables",
    )(pos_f, invf_lane)


def _pre_kernel(x_ref, c_ref, sn_ref, gmix_ref, win_ref, gql_ref, wq_ref, gkvl_ref, wkv_ref,
                gqh_ref, gkh_ref, gsv_ref, wsp_ref, bsp_ref, wpool_ref, pscale_ref,
                gosgu_ref, gopool_ref,
                q_out, k_out, v_out, mr_out, pext_ref):
    j = pl.program_id(1)
    h = _rms(x_ref[...], gmix_ref[...]).astype(BF16)
    z = jnp.dot(h, win_ref[...], preferred_element_type=F32)
    o_kv = Q_LORA
    o_kr = o_kv + KV_LORA
    o_u = o_kr + LANES
    o_v = o_u + SGU_WIDTH
    o_p = o_v + SGU_WIDTH

    c = c_ref[...]
    sn = sn_ref[...]
    lane = lax.broadcasted_iota(jnp.int32, (1, LANES), 1)
    m96 = (lane < QK_HEAD).astype(F32)

    qn_lat = _rms(z[:, 0:Q_LORA], gql_ref[...]).astype(BF16)
    q = jnp.dot(qn_lat, wq_ref[...], preferred_element_type=F32)
    kvn = _rms(z[:, o_kv:o_kr], gkvl_ref[...]).astype(BF16)
    kv = jnp.dot(kvn, wkv_ref[...], preferred_element_type=F32)
    kr = z[:, o_kr:o_u]
    ss_r = jnp.sum(jnp.square(kr * m96), axis=-1, keepdims=True)
    for hh in range(MLA_HEADS):
        lo, hi = hh * HEAD_PAD, (hh + 1) * HEAD_PAD
        qh = q[:, lo:hi]
        ss = jnp.sum(jnp.square(qh * m96), axis=-1, keepdims=True) * (1.0 / QK_HEAD)
        qh = qh * lax.rsqrt(ss + EPS) * gqh_ref[...]
        qrot = pltpu.roll(qh, LANES - ROPE_HALF, 1)
        q_out[hh] = ((qh * c + qrot * sn) * Q_SCALE).astype(BF16)

        knh = kv[:, lo:hi]
        ssk = (jnp.sum(jnp.square(knh), axis=-1, keepdims=True) + ss_r) * (1.0 / QK_HEAD)
        kh = (knh + kr) * lax.rsqrt(ssk + EPS) * gkh_ref[...]
        krot = pltpu.roll(kh, LANES - ROPE_HALF, 1)
        k_out[hh] = (kh * c + krot * sn).astype(BF16)
        v_out[hh] = kv[:, MLA_HEADS * HEAD_PAD + lo:MLA_HEADS * HEAD_PAD + hi].astype(BF16)

    u = z[:, o_u:o_v]
    vs = _rms(z[:, o_v:o_p], gsv_ref[...])
    row = lax.broadcasted_iota(jnp.int32, (CHUNK, SGU_HEADS * CHUNK), 0)
    col = lax.broadcasted_iota(jnp.int32, (CHUNK, SGU_HEADS * CHUNK), 1)
    wt = jnp.where((col & (CHUNK - 1)) <= row, wsp_ref[...], 0.0).astype(BF16)
    lane_head = lax.broadcasted_iota(jnp.int32, (1, SGU_WIDTH), 1) // SGU_HEAD_DIM
    for cc in range(TM // CHUNK):
        r0, r1 = cc * CHUNK, (cc + 1) * CHUNK
        vc = vs[r0:r1, :]
        vbd = jnp.concatenate([jnp.where(lane_head == hh, vc, 0.0) for hh in range(SGU_HEADS)],
                              axis=0).astype(BF16)
        zc = jnp.dot(wt, vbd, preferred_element_type=F32) + bsp_ref[...]
        gm = u[r0:r1, :] * zc
        mr_out[r0:r1, 0:SGU_WIDTH] = _rms(gm, gosgu_ref[...]).astype(BF16)

    pin = z[:, o_p:o_p + POOL_WIDTH]

    @pl.when(j == 0)
    def _():
        pext_ref[0:MAX_WIN, :] = jnp.zeros((MAX_WIN, POOL_WIDTH), F32)

    @pl.when(j != 0)
    def _():
        pext_ref[0:MAX_WIN, :] = pext_ref[TM:TM + MAX_WIN, :]

    pext_ref[MAX_WIN:MAX_WIN + TM, :] = pin

    def shifted(d, lo, hi):
        return pext_ref[MAX_WIN - d:MAX_WIN - d + TM, lo:hi]

    a = [shifted(d, 0, LANES) for d in range(4)]
    s2 = a[0] + a[1]
    s4 = s2 + (a[2] + a[3])
    b = [shifted(d, LANES, 2 * LANES) for d in range(16)]
    s8 = ((b[0] + b[1]) + (b[2] + b[3])) + ((b[4] + b[5]) + (b[6] + b[7]))
    s16 = s8 + (((b[8] + b[9]) + (b[10] + b[11])) + ((b[12] + b[13]) + (b[14] + b[15])))
    first_group = lane < POOL_GROUP_DIM
    wsum = jnp.concatenate([jnp.where(first_group, s2, s4), jnp.where(first_group, s8, s16)], axis=1)
    lane_p = lax.broadcasted_iota(jnp.int32, (1, POOL_WIDTH), 1) // POOL_GROUP_DIM
    win = jnp.where(lane_p == 0, float(POOL_WINDOWS[0]),
                    jnp.where(lane_p == 1, float(POOL_WINDOWS[1]),
                              jnp.where(lane_p == 2, float(POOL_WINDOWS[2]), float(POOL_WINDOWS[3]))))
    t1 = (lax.broadcasted_iota(jnp.int32, (TM, POOL_WIDTH), 0) + (j * TM + 1)).astype(F32)
    count = jnp.minimum(t1, win)
    m = (wsum / count - pin).astype(BF16)
    y = jnp.dot(m, wpool_ref[...], preferred_element_type=F32) * pscale_ref[...]
    mr_out[:, SGU_WIDTH:SGU_WIDTH + POOL_WIDTH] = _rms(y, gopool_ref[...]).astype(BF16)


def _const_spec(shape):
    nd = len(shape)
    return pl.BlockSpec(shape, lambda *_: (0,) * nd)


def _pre_mixer(x, ctab, stab, p):
    B, S, _ = x.shape
    nt = S // TM
    consts = [p["g_mix"], p["w_in"], p["g_ql"], p["w_q"], p["g_kvl"], p["w_kv"], p["g_qh"], p["g_kh"],
              p["g_sv"], p["w_sp"], p["b_sp"], p["w_pool"], p["pscale"], p["g_osgu"], p["g_opool"]]
    head_spec = pl.BlockSpec((None, MLA_HEADS, TM, HEAD_PAD), lambda b, j: (b, 0, j, 0))
    head_shape = jax.ShapeDtypeStruct((B, MLA_HEADS, S, HEAD_PAD), BF16)
    return pl.pallas_call(
        _pre_kernel,
        grid=(B, nt),
        in_specs=[pl.BlockSpec((None, TM, D_MODEL), lambda b, j: (b, j, 0)),
                  pl.BlockSpec((None, TM, LANES), lambda b, j: (b, j, 0)),
                  pl.BlockSpec((None, TM, LANES), lambda b, j: (b, j, 0))]
                 + [_const_spec(a.shape) for a in consts],
        out_specs=[head_spec, head_spec, head_spec,
                   pl.BlockSpec((None, TM, SGU_WIDTH + POOL_WIDTH), lambda b, j: (b, j, 0))],
        out_shape=[head_shape, head_shape, head_shape,
                   jax.ShapeDtypeStruct((B, S, SGU_WIDTH + POOL_WIDTH), BF16)],
        scratch_shapes=[pltpu.VMEM((TM + MAX_WIN, POOL_WIDTH), F32)],
        compiler_params=pltpu.CompilerParams(dimension_semantics=("arbitrary", "arbitrary"),
                                             vmem_limit_bytes=VMEM_LIMIT),
        name="pre_mixer",
    )(x, ctab, stab, *consts)


def _attn_kernel(q_ref, k_ref, v_ref, g_ref, o_ref, m_scr, l_scr, acc_scr, a_scr):
    qi = pl.program_id(1)
    row = lax.broadcasted_iota(jnp.int32, (TQ, TK), 0)
    col = lax.broadcasted_iota(jnp.int32, (TQ, TK), 1)
    causal = col <= row

    def head_body(hh, carry):
        q = q_ref[hh]
        m_scr[...] = jnp.full((TQ, LANES), NEG_BIG, F32)
        l_scr[...] = jnp.zeros((TQ, LANES), F32)
        acc_scr[...] = jnp.zeros((TQ, V_HEAD), F32)

        def kv_step(jj, masked):
            start = pl.multiple_of(jj * TK, TK)
            kblk = k_ref[hh, pl.ds(start, TK), :]
            vblk = v_ref[hh, pl.ds(start, TK), :]
            s = lax.dot_general(q, kblk, (((1,), (1,)), ((), ())), preferred_element_type=F32)
            if masked:
                s = jnp.where(causal, s, NEG_BIG)
            m_prev = m_scr[...]
            m_next = jnp.maximum(m_prev, jnp.max(s, axis=1, keepdims=True))
            p = jnp.exp2(s - jnp.tile(m_next, (1, TK // LANES)))
            alpha = jnp.exp2(m_prev - m_next)
            l_scr[...] = alpha * l_scr[...] + jnp.sum(p, axis=1, keepdims=True)
            acc_scr[...] = acc_scr[...] * alpha + jnp.dot(p.astype(BF16), vblk,
                                                          preferred_element_type=F32)
            m_scr[...] = m_next

        def full_step(jj, c):
            kv_step(jj, False)
            return c

        lax.fori_loop(0, qi, full_step, 0)
        kv_step(qi, True)
        a_scr[hh] = acc_scr[...] / l_scr[...]
        return carry

    lax.fori_loop(0, MLA_HEADS, head_body, 0)
    a = jnp.concatenate([a_scr[hh] for hh in range(MLA_HEADS)], axis=1)
    o_ref[...] = _rms(a, g_ref[...]).astype(BF16)


def _attention(q, k, v, g_out):
    B, H, S, _ = q.shape
    kv_spec = pl.BlockSpec((None, H, S, HEAD_PAD), lambda b, i: (b, 0, 0, 0))
    return pl.pallas_call(
        _attn_kernel,
        grid=(B, S // TQ),
        in_specs=[pl.BlockSpec((None, H, TQ, HEAD_PAD), lambda b, i: (b, 0, i, 0)),
                  kv_spec, kv_spec, _const_spec(g_out.shape)],
        out_specs=pl.BlockSpec((None, TQ, H * V_HEAD), lambda b, i: (b, i, 0)),
        out_shape=jax.ShapeDtypeStruct((B, S, H * V_HEAD), BF16),
        scratch_shapes=[pltpu.VMEM((TQ, LANES), F32), pltpu.VMEM((TQ, LANES), F32),
                        pltpu.VMEM((TQ, V_HEAD), F32), pltpu.VMEM((H, TQ, V_HEAD), F32)],
        compiler_params=pltpu.CompilerParams(dimension_semantics=("parallel", "arbitrary"),
                                             vmem_limit_bytes=VMEM_LIMIT),
        name="mla_attention",
    )(q, k, v, g_out)


def _post_kernel(x_ref, a_ref, mr_ref, woa_ref, wor_ref, gffn_ref, wg_ref, wu_ref, wd_ref,
                 o_ref, h2_scr):
    x1 = (x_ref[...]
          + jnp.dot(a_ref[...], woa_ref[...], preferred_element_type=F32)
          + jnp.dot(mr_ref[...], wor_ref[...], preferred_element_type=F32))
    h2_scr[...] = _rms(x1, gffn_ref[...]).astype(BF16)
    o_ref[...] = x1

    def chunk(cidx, carry):
        h2 = h2_scr[...]
        gate = jnp.dot(h2, wg_ref[cidx], preferred_element_type=F32)
        up = jnp.dot(h2, wu_ref[cidx], preferred_element_type=F32)
        act = (gate * (1.0 / (1.0 + jnp.exp(-gate))) * up).astype(BF16)
        o_ref[...] += jnp.dot(act, wd_ref[cidx], preferred_element_type=F32)
        return carry

    lax.fori_loop(0, N_FFN_CHUNKS, chunk, 0)


def _resident_spec(shape):
    nd = len(shape)
    return pl.BlockSpec(shape, lambda *_: (0,) * nd, pipeline_mode=pl.Buffered(1))


def _post_mixer(x2d, a2d, mr2d, p):
    n = x2d.shape[0]
    weights = [p["w_out_a"], p["w_out_r"], p["g_ffn"], p["w_gate"], p["w_up"], p["w_down"]]
    return pl.pallas_call(
        _post_kernel,
        grid=(n // TM,),
        in_specs=[pl.BlockSpec((TM, D_MODEL), lambda i: (i, 0)),
                  pl.BlockSpec((TM, a2d.shape[1]), lambda i: (i, 0)),
                  pl.BlockSpec((TM, mr2d.shape[1]), lambda i: (i, 0))]
                 + [_resident_spec(w.shape) for w in weights],
        out_specs=pl.BlockSpec((TM, D_MODEL), lambda i: (i, 0)),
        out_shape=jax.ShapeDtypeStruct((n, D_MODEL), F32),
        scratch_shapes=[pltpu.VMEM((TM, D_MODEL), BF16)],
        compiler_params=pltpu.CompilerParams(dimension_semantics=("parallel",),
                                             vmem_limit_bytes=VMEM_LIMIT),
        name="post_mixer",
    )(x2d, a2d, mr2d, *weights)


def _pad_head_gain(g):
    return jnp.concatenate([g, g[QK_NOPE:QK_NOPE + ROPE_HALF],
                            jnp.zeros((LANES - ROPE_COPY_END,), F32)])[None, :]


def _layer_params(l, g_mix_norm, w_in, g_q_lat, w_q_up, g_kv_lat, w_kv_up, g_q_head, g_k_head,
                  g_sgu_v, w_spatial, b_spatial, w_pool, pool_scale, g_out_mla, g_out_sgu,
                  g_out_pool, w_out, g_ffn_norm, w_gate, w_up, w_down):
    o1 = Q_LORA
    o2 = o1 + KV_LORA
    o3 = o2 + QK_ROPE
    o4 = o3 + 2 * SGU_WIDTH
    wi = w_in[l]
    kr = wi[:, o2:o3]
    zeros = functools.partial(jnp.zeros, dtype=F32)
    kr_blk = jnp.concatenate([zeros((D_MODEL, QK_NOPE)), kr, kr[:, :ROPE_HALF],
                              zeros((D_MODEL, LANES - ROPE_COPY_END))], axis=1)
    w_in_p = jnp.concatenate([wi[:, :o2], kr_blk, wi[:, o3:]], axis=1).astype(BF16)

    wq = w_q_up[l].reshape(Q_LORA, MLA_HEADS, QK_HEAD)
    wq_p = jnp.concatenate([wq, wq[:, :, QK_NOPE:QK_NOPE + ROPE_HALF],
                            zeros((Q_LORA, MLA_HEADS, LANES - ROPE_COPY_END))], axis=2)
    wq_p = wq_p.reshape(Q_LORA, MLA_HEADS * HEAD_PAD).astype(BF16)

    wkv = w_kv_up[l].reshape(KV_LORA, MLA_HEADS, QK_NOPE + V_HEAD)
    wkn = jnp.concatenate([wkv[:, :, :QK_NOPE], zeros((KV_LORA, MLA_HEADS, HEAD_PAD - QK_NOPE))], axis=2)
    wkv_p = jnp.concatenate([wkn.reshape(KV_LORA, MLA_HEADS * HEAD_PAD),
                             wkv[:, :, QK_NOPE:].reshape(KV_LORA, MLA_HEADS * V_HEAD)],
                            axis=1).astype(BF16)

    w_sp = jnp.transpose(w_spatial[l], (1, 0, 2)).reshape(CHUNK, SGU_HEADS * CHUNK)
    b_sp = jnp.repeat(b_spatial[l].T, SGU_HEAD_DIM, axis=1)
    w_pool_bd = jax.scipy.linalg.block_diag(*[w_pool[l, g] for g in range(len(POOL_WINDOWS))]).astype(BF16)

    chunked = lambda w: jnp.transpose(w.reshape(D_MODEL, N_FFN_CHUNKS, FFN_CHUNK), (1, 0, 2)).astype(BF16)
    return dict(
        g_mix=g_mix_norm[l][None, :], w_in=w_in_p, g_ql=g_q_lat[l][None, :], w_q=wq_p,
        g_kvl=g_kv_lat[l][None, :], w_kv=wkv_p,
        g_qh=_pad_head_gain(g_q_head[l]), g_kh=_pad_head_gain(g_k_head[l]),
        g_sv=g_sgu_v[l][None, :], w_sp=w_sp, b_sp=b_sp, w_pool=w_pool_bd,
        pscale=pool_scale[l][None, :], g_osgu=g_out_sgu[l][None, :], g_opool=g_out_pool[l][None, :],
        g_omla=g_out_mla[l][None, :],
        w_out_a=w_out[l, :MLA_HEADS * V_HEAD].astype(BF16), w_out_r=w_out[l, MLA_HEADS * V_HEAD:].astype(BF16),
        g_ffn=g_ffn_norm[l][None, :], w_gate=chunked(w_gate[l]), w_up=chunked(w_up[l]),
        w_down=w_down[l].reshape(N_FFN_CHUNKS, FFN_CHUNK, D_MODEL).astype(BF16),
    )


def kernel(x, positions, g_mix_norm, w_in, g_q_lat, w_q_up, g_kv_lat, w_kv_up, g_q_head, g_k_head, g_sgu_v, w_spatial, b_spatial, w_pool, pool_scale, g_out_mla, g_out_sgu, g_out_pool, w_out, g_ffn_norm, w_gate, w_up, w_down):
    B, S, D = x.shape
    depth = w_in.shape[0]
    assert D == D_MODEL and S % TM == 0 and S % TQ == 0 and TQ == TK and TM % CHUNK == 0

    inv_freq = 1.0 / (ROPE_THETA ** (jnp.arange(ROPE_HALF, dtype=F32) / ROPE_HALF))
    invf_lane = jnp.concatenate([jnp.zeros((QK_NOPE,), F32), inv_freq, inv_freq,
                                 jnp.zeros((LANES - QK_HEAD,), F32)])[None, :]
    ctab, stab = _rope_tables(positions.astype(F32).reshape(B * S, 1), invf_lane)
    ctab = ctab.reshape(B, S, LANES)
    stab = stab.reshape(B, S, LANES)

    for l in range(depth):
        p = _layer_params(l, g_mix_norm, w_in, g_q_lat, w_q_up, g_kv_lat, w_kv_up, g_q_head, g_k_head,
                          g_sgu_v, w_spatial, b_spatial, w_pool, pool_scale, g_out_mla, g_out_sgu,
                          g_out_pool, w_out, g_ffn_norm, w_gate, w_up, w_down)
        q, k, v, mr = _pre_mixer(x, ctab, stab, p)
        a = _attention(q, k, v, p["g_omla"])
        x = _post_mixer(x.reshape(B * S, D), a.reshape(B * S, -1), mr.reshape(B * S, -1), p).reshape(B, S, D)
    return x
```

```python
import functools
import math

import jax
import jax.numpy as jnp
from jax import lax
from jax.experimental import pallas as pl
from jax.experimental.pallas import tpu as pltpu

D_MODEL = 1024
MLA_HEADS = 4
V_HEAD = 128
QK_NOPE = 64
QK_ROPE = 32
QK_HEAD = QK_NOPE + QK_ROPE
Q_LORA = 256
KV_LORA = 128
ROPE_THETA = 10000.0
SGU_WIDTH = 256
SGU_HEADS = 4
SGU_HEAD_DIM = 64
CHUNK = 128
POOL_WIDTH = 256
POOL_WINDOWS = (2, 4, 8, 16)
POOL_GROUP_DIM = 64
MAX_WIN = max(POOL_WINDOWS)
FFN_HIDDEN = 2816
EPS = 1e-6

LANES = 128
HEAD_PAD = LANES
IN_PAD = Q_LORA + KV_LORA + LANES + 2 * SGU_WIDTH + POOL_WIDTH
ROPE_HALF = QK_ROPE // 2
ROPE_COPY_END = QK_HEAD + ROPE_HALF
Q_SCALE = math.log2(math.e) / math.sqrt(QK_HEAD)
NEG_BIG = -1e30

TM = 512
TQ = 512
TK = 512
FFN_CHUNK = 256
N_FFN_CHUNKS = FFN_HIDDEN // FFN_CHUNK
VMEM_LIMIT = 56 * 1024 * 1024

F32 = jnp.float32
BF16 = jnp.bfloat16


def _rms(v, g):
    return v * lax.rsqrt(jnp.mean(v * v, axis=-1, keepdims=True) + EPS) * g


def _rope_table_kernel(pos_ref, invf_ref, c_ref, s_ref):
    ang = pos_ref[...] * invf_ref[...]
    lane = lax.broadcasted_iota(jnp.int32, ang.shape, 1)
    cosv = jnp.cos(ang)
    sinv = jnp.sin(ang)
    c_ref[...] = jnp.where(lane < QK_NOPE, 1.0, jnp.where(lane < QK_HEAD, cosv, 0.0))
    first = (lane >= QK_NOPE) & (lane < QK_NOPE + ROPE_HALF)
    second = (lane >= QK_NOPE + ROPE_HALF) & (lane < QK_HEAD)
    s_ref[...] = jnp.where(first, -sinv, jnp.where(second, sinv, 0.0))


def _rope_tables(pos_f, invf_lane):
    n = pos_f.shape[0]
    return pl.pallas_call(
        _rope_table_kernel,
        grid=(n // TM,),
        in_specs=[pl.BlockSpec((TM, 1), lambda i: (i, 0)),
                  pl.BlockSpec((1, LANES), lambda i: (0, 0))],
        out_specs=[pl.BlockSpec((TM, LANES), lambda i: (i, 0)),
                   pl.BlockSpec((TM, LANES), lambda i: (i, 0))],
        out_shape=[jax.ShapeDtypeStruct((n, LANES), F32)] * 2,
        compiler_params=pltpu.CompilerParams(dimension_semantics=("parallel",)),
        name="rope_tables",
    )(pos_f, invf_lane)


def _pre_kernel(x_ref, c_ref, sn_ref, gmix_ref, win_ref, gql_ref, wq_ref, gkvl_ref, wkn_ref, wvt_ref,
                gqh_ref, gkh_ref, gsv_ref, wsp_ref, bsp_ref, wpool_ref, pscale_ref,
                gosgu_ref, gopool_ref,
                q_out, k_out, v_out, mr_out, pext_ref):
    j = pl.program_id(1)
    h = _rms(x_ref[...], gmix_ref[...]).astype(BF16)
    z = jnp.dot(h, win_ref[...], preferred_element_type=F32)
    o_kv = Q_LORA
    o_kr = o_kv + KV_LORA
    o_u = o_kr + LANES
    o_v = o_u + SGU_WIDTH
    o_p = o_v + SGU_WIDTH

    c = c_ref[...]
    sn = sn_ref[...]
    lane = lax.broadcasted_iota(jnp.int32, (1, LANES), 1)
    m96 = (lane < QK_HEAD).astype(F32)

    qn_lat = _rms(z[:, 0:Q_LORA], gql_ref[...]).astype(BF16)
    q = jnp.dot(qn_lat, wq_ref[...], preferred_element_type=F32)
    kvn = _rms(z[:, o_kv:o_kr], gkvl_ref[...]).astype(BF16)
    kv = jnp.dot(kvn, wkn_ref[...], preferred_element_type=F32)
    vt = lax.dot_general(wvt_ref[...], kvn, (((1,), (1,)), ((), ())), preferred_element_type=F32)
    kr = z[:, o_kr:o_u]
    ss_r = jnp.sum(jnp.square(kr * m96), axis=-1, keepdims=True)
    for hh in range(MLA_HEADS):
        lo, hi = hh * HEAD_PAD, (hh + 1) * HEAD_PAD
        qh = q[:, lo:hi]
        ss = jnp.sum(jnp.square(qh * m96), axis=-1, keepdims=True) * (1.0 / QK_HEAD)
        qh = qh * lax.rsqrt(ss + EPS) * gqh_ref[...]
        qrot = pltpu.roll(qh, LANES - ROPE_HALF, 1)
        q_out[hh] = ((qh * c + qrot * sn) * Q_SCALE).astype(BF16)

        knh = kv[:, lo:hi]
        ssk = (jnp.sum(jnp.square(knh), axis=-1, keepdims=True) + ss_r) * (1.0 / QK_HEAD)
        kh = (knh + kr) * lax.rsqrt(ssk + EPS) * gkh_ref[...]
        krot = pltpu.roll(kh, LANES - ROPE_HALF, 1)
        k_out[hh] = (kh * c + krot * sn).astype(BF16)
        v_out[hh] = vt[hh * V_HEAD:(hh + 1) * V_HEAD, :].astype(BF16)

    u = z[:, o_u:o_v]
    vs = _rms(z[:, o_v:o_p], gsv_ref[...])
    row = lax.broadcasted_iota(jnp.int32, (CHUNK, SGU_HEADS * CHUNK), 0)
    col = lax.broadcasted_iota(jnp.int32, (CHUNK, SGU_HEADS * CHUNK), 1)
    wt = jnp.where((col & (CHUNK - 1)) <= row, wsp_ref[...], 0.0).astype(BF16)
    lane_head = lax.broadcasted_iota(jnp.int32, (1, SGU_WIDTH), 1) // SGU_HEAD_DIM
    for cc in range(TM // CHUNK):
        r0, r1 = cc * CHUNK, (cc + 1) * CHUNK
        vc = vs[r0:r1, :]
        vbd = jnp.concatenate([jnp.where(lane_head == hh, vc, 0.0) for hh in range(SGU_HEADS)],
                              axis=0).astype(BF16)
        zc = jnp.dot(wt, vbd, preferred_element_type=F32) + bsp_ref[...]
        gm = u[r0:r1, :] * zc
        mr_out[r0:r1, 0:SGU_WIDTH] = _rms(gm, gosgu_ref[...]).astype(BF16)

    pin = z[:, o_p:o_p + POOL_WIDTH]

    @pl.when(j == 0)
    def _():
        pext_ref[0:MAX_WIN, :] = jnp.zeros((MAX_WIN, POOL_WIDTH), F32)

    @pl.when(j != 0)
    def _():
        pext_ref[0:MAX_WIN, :] = pext_ref[TM:TM + MAX_WIN, :]

    pext_ref[MAX_WIN:MAX_WIN + TM, :] = pin

    def shifted(d, lo, hi):
        return pext_ref[MAX_WIN - d:MAX_WIN - d + TM, lo:hi]

    a = [shifted(d, 0, LANES) for d in range(4)]
    s2 = a[0] + a[1]
    s4 = s2 + (a[2] + a[3])
    b = [shifted(d, LANES, 2 * LANES) for d in range(16)]
    s8 = ((b[0] + b[1]) + (b[2] + b[3])) + ((b[4] + b[5]) + (b[6] + b[7]))
    s16 = s8 + (((b[8] + b[9]) + (b[10] + b[11])) + ((b[12] + b[13]) + (b[14] + b[15])))
    first_group = lane < POOL_GROUP_DIM
    wsum = jnp.concatenate([jnp.where(first_group, s2, s4), jnp.where(first_group, s8, s16)], axis=1)
    lane_p = lax.broadcasted_iota(jnp.int32, (1, POOL_WIDTH), 1) // POOL_GROUP_DIM
    win = jnp.where(lane_p == 0, float(POOL_WINDOWS[0]),
                    jnp.where(lane_p == 1, float(POOL_WINDOWS[1]),
                              jnp.where(lane_p == 2, float(POOL_WINDOWS[2]), float(POOL_WINDOWS[3]))))
    t1 = (lax.broadcasted_iota(jnp.int32, (TM, POOL_WIDTH), 0) + (j * TM + 1)).astype(F32)
    count = jnp.minimum(t1, win)
    m = (wsum / count - pin).astype(BF16)
    y = jnp.dot(m, wpool_ref[...], preferred_element_type=F32) * pscale_ref[...]
    mr_out[:, SGU_WIDTH:SGU_WIDTH + POOL_WIDTH] = _rms(y, gopool_ref[...]).astype(BF16)


def _const_spec(shape):
    nd = len(shape)
    return pl.BlockSpec(shape, lambda *_: (0,) * nd)


def _pre_mixer(x, ctab, stab, p):
    B, S, _ = x.shape
    nt = S // TM
    consts = [p["g_mix"], p["w_in"], p["g_ql"], p["w_q"], p["g_kvl"], p["w_kn"], p["w_vt"], p["g_qh"], p["g_kh"],
              p["g_sv"], p["w_sp"], p["b_sp"], p["w_pool"], p["pscale"], p["g_osgu"], p["g_opool"]]
    head_spec = pl.BlockSpec((None, MLA_HEADS, TM, HEAD_PAD), lambda b, j: (b, 0, j, 0))
    head_shape = jax.ShapeDtypeStruct((B, MLA_HEADS, S, HEAD_PAD), BF16)
    return pl.pallas_call(
        _pre_kernel,
        grid=(B, nt),
        in_specs=[pl.BlockSpec((None, TM, D_MODEL), lambda b, j: (b, j, 0)),
                  pl.BlockSpec((None, TM, LANES), lambda b, j: (b, j, 0)),
                  pl.BlockSpec((None, TM, LANES), lambda b, j: (b, j, 0))]
                 + [_const_spec(a.shape) for a in consts],
        out_specs=[head_spec, head_spec,
                   pl.BlockSpec((None, MLA_HEADS, None, V_HEAD, TM), lambda b, j: (b, 0, j, 0, 0)),
                   pl.BlockSpec((None, TM, SGU_WIDTH + POOL_WIDTH), lambda b, j: (b, j, 0))],
        out_shape=[head_shape, head_shape,
                   jax.ShapeDtypeStruct((B, MLA_HEADS, nt, V_HEAD, TM), BF16),
                   jax.ShapeDtypeStruct((B, S, SGU_WIDTH + POOL_WIDTH), BF16)],
        scratch_shapes=[pltpu.VMEM((TM + MAX_WIN, POOL_WIDTH), F32)],
        compiler_params=pltpu.CompilerParams(dimension_semantics=("arbitrary", "arbitrary"),
                                             vmem_limit_bytes=VMEM_LIMIT),
        name="pre_mixer",
    )(x, ctab, stab, *consts)


def _attn_kernel(q_ref, k_ref, v_ref, g_ref, o_ref, m_scr, l_scr, acc_scr):
    qi = pl.program_id(1)
    key = lax.broadcasted_iota(jnp.int32, (TK, TQ), 0)
    qry = lax.broadcasted_iota(jnp.int32, (TK, TQ), 1)
    causal = key <= qry
    m_scr[...] = jnp.full(m_scr.shape, NEG_BIG, F32)
    l_scr[...] = jnp.zeros(l_scr.shape, F32)
    acc_scr[...] = jnp.zeros(acc_scr.shape, F32)

    def scores(hh, jj):
        kblk = k_ref[hh, pl.ds(pl.multiple_of(jj * TK, TK), TK), :]
        return lax.dot_general(kblk, q_ref[hh], (((1,), (1,)), ((), ())),
                               preferred_element_type=F32)

    def softmax_update(hh, s):
        m_prev = m_scr[hh]
        m_next = jnp.maximum(m_prev, jnp.max(s, axis=0, keepdims=True))
        p = jnp.exp2(s - m_next)
        alpha = jnp.exp2(m_prev - m_next)
        l_scr[hh] = alpha * l_scr[hh] + jnp.sum(p, axis=0, keepdims=True)
        m_scr[hh] = m_next
        return p.astype(BF16), alpha

    def accumulate(hh, jj, p, alpha):
        acc_scr[hh] = acc_scr[hh] * alpha + jnp.dot(v_ref[hh, jj], p,
                                                    preferred_element_type=F32)

    def kv_step(jj, s_first, masked):
        s_next = s_first
        for hh in range(MLA_HEADS):
            s_cur = s_next
            if hh + 1 < MLA_HEADS:
                s_next = scores(hh + 1, jj)
            elif not masked:
                s_next = scores(0, jj + 1)
            if masked:
                s_cur = jnp.where(causal, s_cur, NEG_BIG)
            p, alpha = softmax_update(hh, s_cur)
            accumulate(hh, jj, p, alpha)
        return s_next

    s_diag = lax.fori_loop(0, qi, lambda jj, s: kv_step(jj, s, False), scores(0, 0))
    kv_step(qi, s_diag, True)
    a = jnp.concatenate([(acc_scr[hh] / l_scr[hh]).T for hh in range(MLA_HEADS)], axis=1)
    o_ref[...] = _rms(a, g_ref[...]).astype(BF16)


def _attention(q, k, v, g_out):
    B, H, S, _ = q.shape
    k_spec = pl.BlockSpec((None, H, S, HEAD_PAD), lambda b, i: (b, 0, 0, 0))
    v_spec = pl.BlockSpec((None, H, S // TK, V_HEAD, TK), lambda b, i: (b, 0, 0, 0, 0))
    return pl.pallas_call(
        _attn_kernel,
        grid=(B, S // TQ),
        in_specs=[pl.BlockSpec((None, H, TQ, HEAD_PAD), lambda b, i: (b, 0, i, 0)),
                  k_spec, v_spec, _const_spec(g_out.shape)],
        out_specs=pl.BlockSpec((None, TQ, H * V_HEAD), lambda b, i: (b, i, 0)),
        out_shape=jax.ShapeDtypeStruct((B, S, H * V_HEAD), BF16),
        scratch_shapes=[pltpu.VMEM((H, 1, TQ), F32), pltpu.VMEM((H, 1, TQ), F32),
                        pltpu.VMEM((H, V_HEAD, TQ), F32)],
        compiler_params=pltpu.CompilerParams(dimension_semantics=("parallel", "arbitrary"),
                                             vmem_limit_bytes=VMEM_LIMIT),
        name="mla_attention",
    )(q, k, v, g_out)


def _post_kernel(x_ref, a_ref, mr_ref, woa_ref, wor_ref, gffn_ref, wg_ref, wu_ref, wd_ref,
                 o_ref, h2_scr):
    x1 = (x_ref[...]
          + jnp.dot(a_ref[...], woa_ref[...], preferred_element_type=F32)
          + jnp.dot(mr_ref[...], wor_ref[...], preferred_element_type=F32))
    h2_scr[...] = _rms(x1, gffn_ref[...]).astype(BF16)
    o_ref[...] = x1

    for cidx in range(N_FFN_CHUNKS):
        h2 = h2_scr[...]
        gate = jnp.dot(h2, wg_ref[cidx], preferred_element_type=F32)
        up = jnp.dot(h2, wu_ref[cidx], preferred_element_type=F32)
        act = (gate * (1.0 / (1.0 + jnp.exp(-gate))) * up).astype(BF16)
        o_ref[...] += jnp.dot(act, wd_ref[cidx], preferred_element_type=F32)


def _resident_spec(shape):
    nd = len(shape)
    return pl.BlockSpec(shape, lambda *_: (0,) * nd, pipeline_mode=pl.Buffered(1))


def _post_mixer(x2d, a2d, mr2d, p):
    n = x2d.shape[0]
    weights = [p["w_out_a"], p["w_out_r"], p["g_ffn"], p["w_gate"], p["w_up"], p["w_down"]]
    return pl.pallas_call(
        _post_kernel,
        grid=(n // TM,),
        in_specs=[pl.BlockSpec((TM, D_MODEL), lambda i: (i, 0)),
                  pl.BlockSpec((TM, a2d.shape[1]), lambda i: (i, 0)),
                  pl.BlockSpec((TM, mr2d.shape[1]), lambda i: (i, 0))]
                 + [_resident_spec(w.shape) for w in weights],
        out_specs=pl.BlockSpec((TM, D_MODEL), lambda i: (i, 0)),
        out_shape=jax.ShapeDtypeStruct((n, D_MODEL), F32),
        scratch_shapes=[pltpu.VMEM((TM, D_MODEL), BF16)],
        compiler_params=pltpu.CompilerParams(dimension_semantics=("parallel",),
                                             vmem_limit_bytes=VMEM_LIMIT),
        name="post_mixer",
    )(x2d, a2d, mr2d, *weights)


def _pad_head_gain(g):
    return jnp.concatenate([g, g[QK_NOPE:QK_NOPE + ROPE_HALF],
                            jnp.zeros((LANES - ROPE_COPY_END,), F32)])[None, :]


def _layer_params(l, g_mix_norm, w_in, g_q_lat, w_q_up, g_kv_lat, w_kv_up, g_q_head, g_k_head,
                  g_sgu_v, w_spatial, b_spatial, w_pool, pool_scale, g_out_mla, g_out_sgu,
                  g_out_pool, w_out, g_ffn_norm, w_gate, w_up, w_down):
    o1 = Q_LORA
    o2 = o1 + KV_LORA
    o3 = o2 + QK_ROPE
    o4 = o3 + 2 * SGU_WIDTH
    wi = w_in[l]
    kr = wi[:, o2:o3]
    zeros = functools.partial(jnp.zeros, dtype=F32)
    kr_blk = jnp.concatenate([zeros((D_MODEL, QK_NOPE)), kr, kr[:, :ROPE_HALF],
                              zeros((D_MODEL, LANES - ROPE_COPY_END))], axis=1)
    w_in_p = jnp.concatenate([wi[:, :o2], kr_blk, wi[:, o3:]], axis=1).astype(BF16)

    wq = w_q_up[l].reshape(Q_LORA, MLA_HEADS, QK_HEAD)
    wq_p = jnp.concatenate([wq, wq[:, :, QK_NOPE:QK_NOPE + ROPE_HALF],
                            zeros((Q_LORA, MLA_HEADS, LANES - ROPE_COPY_END))], axis=2)
    wq_p = wq_p.reshape(Q_LORA, MLA_HEADS * HEAD_PAD).astype(BF16)

    wkv = w_kv_up[l].reshape(KV_LORA, MLA_HEADS, QK_NOPE + V_HEAD)
    wkn = jnp.concatenate([wkv[:, :, :QK_NOPE], zeros((KV_LORA, MLA_HEADS, HEAD_PAD - QK_NOPE))], axis=2)
    wkn_p = wkn.reshape(KV_LORA, MLA_HEADS * HEAD_PAD).astype(BF16)
    wvt_p = wkv[:, :, QK_NOPE:].reshape(KV_LORA, MLA_HEADS * V_HEAD).T.astype(BF16)

    w_sp = jnp.transpose(w_spatial[l], (1, 0, 2)).reshape(CHUNK, SGU_HEADS * CHUNK)
    b_sp = jnp.repeat(b_spatial[l].T, SGU_HEAD_DIM, axis=1)
    w_pool_bd = jax.scipy.linalg.block_diag(*[w_pool[l, g] for g in range(len(POOL_WINDOWS))]).astype(BF16)

    chunked = lambda w: jnp.transpose(w.reshape(D_MODEL, N_FFN_CHUNKS, FFN_CHUNK), (1, 0, 2)).astype(BF16)
    return dict(
        g_mix=g_mix_norm[l][None, :], w_in=w_in_p, g_ql=g_q_lat[l][None, :], w_q=wq_p,
        g_kvl=g_kv_lat[l][None, :], w_kn=wkn_p, w_vt=wvt_p,
        g_qh=_pad_head_gain(g_q_head[l]), g_kh=_pad_head_gain(g_k_head[l]),
        g_sv=g_sgu_v[l][None, :], w_sp=w_sp, b_sp=b_sp, w_pool=w_pool_bd,
        pscale=pool_scale[l][None, :], g_osgu=g_out_sgu[l][None, :], g_opool=g_out_pool[l][None, :],
        g_omla=g_out_mla[l][None, :],
        w_out_a=w_out[l, :MLA_HEADS * V_HEAD].astype(BF16), w_out_r=w_out[l, MLA_HEADS * V_HEAD:].astype(BF16),
        g_ffn=g_ffn_norm[l][None, :], w_gate=chunked(w_gate[l]), w_up=chunked(w_up[l]),
        w_down=w_down[l].reshape(N_FFN_CHUNKS, FFN_CHUNK, D_MODEL).astype(BF16),
    )


def kernel(x, positions, g_mix_norm, w_in, g_q_lat, w_q_up, g_kv_lat, w_kv_up, g_q_head, g_k_head, g_sgu_v, w_spatial, b_spatial, w_pool, pool_scale, g_out_mla, g_out_sgu, g_out_pool, w_out, g_ffn_norm, w_gate, w_up, w_down):
    B, S, D = x.shape
    depth = w_in.shape[0]
    assert D == D_MODEL and S % TM == 0 and S % TQ == 0 and TQ == TK and TK == TM and TM % CHUNK == 0

    inv_freq = 1.0 / (ROPE_THETA ** (jnp.arange(ROPE_HALF, dtype=F32) / ROPE_HALF))
    invf_lane = jnp.concatenate([jnp.zeros((QK_NOPE,), F32), inv_freq, inv_freq,
                                 jnp.zeros((LANES - QK_HEAD,), F32)])[None, :]
    ctab, stab = _rope_tables(positions.astype(F32).reshape(B * S, 1), invf_lane)
    ctab = ctab.reshape(B, S, LANES)
    stab = stab.reshape(B, S, LANES)

    for l in range(depth):
        p = _layer_params(l, g_mix_norm, w_in, g_q_lat, w_q_up, g_kv_lat, w_kv_up, g_q_head, g_k_head,
                          g_sgu_v, w_spatial, b_spatial, w_pool, pool_scale, g_out_mla, g_out_sgu,
                          g_out_pool, w_out, g_ffn_norm, w_gate, w_up, w_down)
        q, k, v, mr = _pre_mixer(x, ctab, stab, p)
        a = _attention(q, k, v, p["g_omla"])
        x = _post_mixer(x.reshape(B * S, D), a.reshape(B * S, -1), mr.reshape(B * S, -1), p).reshape(B, S, D)
    return x
```

```python
import functools
import math

import jax
import jax.numpy as jnp
from jax import lax
from jax.experimental import pallas as pl
from jax.experimental.pallas import tpu as pltpu

D_MODEL = 1024
MLA_HEADS = 4
V_HEAD = 128
QK_NOPE = 64
QK_ROPE = 32
QK_HEAD = QK_NOPE + QK_ROPE
Q_LORA = 256
KV_LORA = 128
ROPE_THETA = 10000.0
SGU_WIDTH = 256
SGU_HEADS = 4
SGU_HEAD_DIM = 64
CHUNK = 128
POOL_WIDTH = 256
POOL_WINDOWS = (2, 4, 8, 16)
POOL_GROUP_DIM = 64
MAX_WIN = max(POOL_WINDOWS)
FFN_HIDDEN = 2816
EPS = 1e-6

LANES = 128
HEAD_PAD = LANES
QK_WIDTH = MLA_HEADS * HEAD_PAD
IN_PAD = Q_LORA + KV_LORA + LANES + 2 * SGU_WIDTH + POOL_WIDTH
ROPE_HALF = QK_ROPE // 2
ROPE_COPY_END = QK_HEAD + ROPE_HALF
Q_SCALE = math.log2(math.e) / math.sqrt(QK_HEAD)
NEG_BIG = -1e30

TM = 512
TQ = 512
TK = 512
FFN_CHUNK = 256
N_FFN_CHUNKS = FFN_HIDDEN // FFN_CHUNK
VMEM_LIMIT = 56 * 1024 * 1024

F32 = jnp.float32
BF16 = jnp.bfloat16


def _rms(v, g):
    return v * lax.rsqrt(jnp.mean(v * v, axis=-1, keepdims=True) + EPS) * g


def _layer_spec(shape, layer, **kw):
    nd = len(shape) - 1
    return pl.BlockSpec((None,) + tuple(shape[1:]), lambda *_: (layer,) + (0,) * nd, **kw)


def _const_spec(shape):
    nd = len(shape)
    return pl.BlockSpec(shape, lambda *_: (0,) * nd)


def _rope_table_kernel(pos_ref, invf_ref, c_ref, s_ref):
    ang = pos_ref[...] * invf_ref[...]
    lane = lax.broadcasted_iota(jnp.int32, ang.shape, 1)
    cosv = jnp.cos(ang)
    sinv = jnp.sin(ang)
    c_ref[...] = jnp.where(lane < QK_NOPE, 1.0, jnp.where(lane < QK_HEAD, cosv, 0.0))
    first = (lane >= QK_NOPE) & (lane < QK_NOPE + ROPE_HALF)
    second = (lane >= QK_NOPE + ROPE_HALF) & (lane < QK_HEAD)
    s_ref[...] = jnp.where(first, -sinv, jnp.where(second, sinv, 0.0))


def _rope_tables(pos_f, invf_lane):
    n = pos_f.shape[0]
    return pl.pallas_call(
        _rope_table_kernel,
        grid=(n // TM,),
        in_specs=[pl.BlockSpec((TM, 1), lambda i: (i, 0)),
                  pl.BlockSpec((1, LANES), lambda i: (0, 0))],
        out_specs=[pl.BlockSpec((TM, LANES), lambda i: (i, 0)),
                   pl.BlockSpec((TM, LANES), lambda i: (i, 0))],
        out_shape=[jax.ShapeDtypeStruct((n, LANES), F32)] * 2,
        compiler_params=pltpu.CompilerParams(dimension_semantics=("parallel",)),
        name="rope_tables",
    )(pos_f, invf_lane)


def _pre_kernel(x_ref, c_ref, sn_ref, ehead_ref, gmix_ref, win_ref, gql_ref, wq_ref, wqr_ref, gkvl_ref,
                wkn_ref, wvt_ref, gqh_ref, gqr_ref, gkh_ref, gkr_ref, gsv_ref, wsp_ref, bsp_ref,
                wpool_ref, pscale_ref, gosgu_ref, gopool_ref,
                q_out, k_out, v_out, mr_out, pext_ref):
    j = pl.program_id(1)
    h = _rms(x_ref[...], gmix_ref[...]).astype(BF16)
    z = jnp.dot(h, win_ref[...], preferred_element_type=F32)
    o_kv = Q_LORA
    o_kr = o_kv + KV_LORA
    o_u = o_kr + LANES
    o_v = o_u + SGU_WIDTH
    o_p = o_v + SGU_WIDTH

    c = c_ref[...]
    sn = sn_ref[...]
    ehead = ehead_ref[...]

    qn_lat = _rms(z[:, 0:Q_LORA], gql_ref[...]).astype(BF16)
    q = jnp.dot(qn_lat, wq_ref[...], preferred_element_type=F32)
    q_sw = jnp.dot(qn_lat, wqr_ref[...], preferred_element_type=F32)
    rq = lax.rsqrt(jnp.dot((q * q).astype(BF16), ehead, preferred_element_type=F32) * (1.0 / QK_HEAD) + EPS)

    kvn = _rms(z[:, o_kv:o_kr], gkvl_ref[...]).astype(BF16)
    kn = jnp.dot(kvn, wkn_ref[...], preferred_element_type=F32)
    vt = lax.dot_general(wvt_ref[...], kvn, (((1,), (1,)), ((), ())), preferred_element_type=F32)
    kr = z[:, o_kr:o_u]
    kr_sw = pltpu.roll(kr, LANES - ROPE_HALF, 1)
    kf = jnp.concatenate([kn[:, hh * HEAD_PAD:(hh + 1) * HEAD_PAD] + kr for hh in range(MLA_HEADS)], axis=1)
    rk = lax.rsqrt(jnp.dot((kf * kf).astype(BF16), ehead, preferred_element_type=F32) * (1.0 / QK_HEAD) + EPS)
    for hh in range(MLA_HEADS):
        sl = slice(hh * HEAD_PAD, (hh + 1) * HEAD_PAD)
        r = rq[:, sl]
        q_out[hh] = (((q[:, sl] * r * gqh_ref[...]) * c + (q_sw[:, sl] * r * gqr_ref[...]) * sn)
                     * Q_SCALE).astype(BF16)
        r = rk[:, sl]
        k_out[hh] = ((kf[:, sl] * r * gkh_ref[...]) * c + (kr_sw * r * gkr_ref[...]) * sn).astype(BF16)
        v_out[hh] = vt[hh * V_HEAD:(hh + 1) * V_HEAD, :].astype(BF16)

    u = z[:, o_u:o_v]
    vs = _rms(z[:, o_v:o_p], gsv_ref[...])
    row = lax.broadcasted_iota(jnp.int32, (CHUNK, SGU_HEADS * CHUNK), 0)
    col = lax.broadcasted_iota(jnp.int32, (CHUNK, SGU_HEADS * CHUNK), 1)
    wt = jnp.where((col & (CHUNK - 1)) <= row, wsp_ref[...], 0.0).astype(BF16)
    lane_head = lax.broadcasted_iota(jnp.int32, (1, SGU_WIDTH), 1) // SGU_HEAD_DIM
    for cc in range(TM // CHUNK):
        r0, r1 = cc * CHUNK, (cc + 1) * CHUNK
        vc = vs[r0:r1, :]
        vbd = jnp.concatenate([jnp.where(lane_head == hh, vc, 0.0) for hh in range(SGU_HEADS)],
                              axis=0).astype(BF16)
        zc = jnp.dot(wt, vbd, preferred_element_type=F32) + bsp_ref[...]
        gm = u[r0:r1, :] * zc
        mr_out[r0:r1, 0:SGU_WIDTH] = _rms(gm, gosgu_ref[...]).astype(BF16)

    pin = z[:, o_p:o_p + POOL_WIDTH]

    @pl.when(j == 0)
    def _():
        pext_ref[0:MAX_WIN, :] = jnp.zeros((MAX_WIN, POOL_WIDTH), F32)

    @pl.when(j != 0)
    def _():
        pext_ref[0:MAX_WIN, :] = pext_ref[TM:TM + MAX_WIN, :]

    pext_ref[MAX_WIN:MAX_WIN + TM, :] = pin

    def shifted(d, lo, hi):
        return pext_ref[MAX_WIN - d:MAX_WIN - d + TM, lo:hi]

    lane = lax.broadcasted_iota(jnp.int32, (1, LANES), 1)
    a = [shifted(d, 0, LANES) for d in range(4)]
    s2 = a[0] + a[1]
    s4 = s2 + (a[2] + a[3])
    b = [shifted(d, LANES, 2 * LANES) for d in range(16)]
    s8 = ((b[0] + b[1]) + (b[2] + b[3])) + ((b[4] + b[5]) + (b[6] + b[7]))
    s16 = s8 + (((b[8] + b[9]) + (b[10] + b[11])) + ((b[12] + b[13]) + (b[14] + b[15])))
    first_group = lane < POOL_GROUP_DIM
    wsum = jnp.concatenate([jnp.where(first_group, s2, s4), jnp.where(first_group, s8, s16)], axis=1)
    lane_p = lax.broadcasted_iota(jnp.int32, (1, POOL_WIDTH), 1) // POOL_GROUP_DIM
    win = jnp.where(lane_p == 0, float(POOL_WINDOWS[0]),
                    jnp.where(lane_p == 1, float(POOL_WINDOWS[1]),
                              jnp.where(lane_p == 2, float(POOL_WINDOWS[2]), float(POOL_WINDOWS[3]))))
    t1 = (lax.broadcasted_iota(jnp.int32, (TM, POOL_WIDTH), 0) + (j * TM + 1)).astype(F32)
    count = jnp.minimum(t1, win)
    m = (wsum / count - pin).astype(BF16)
    y = jnp.dot(m, wpool_ref[...], preferred_element_type=F32) * pscale_ref[...]
    mr_out[:, SGU_WIDTH:SGU_WIDTH + POOL_WIDTH] = _rms(y, gopool_ref[...]).astype(BF16)


def _pre_mixer(x, ctab, stab, ehead, p, layer):
    B, S, _ = x.shape
    nt = S // TM
    names = ["g_mix", "w_in", "g_ql", "w_q", "w_qr", "g_kvl", "w_kn", "w_vt", "g_qh", "g_qr", "g_kh", "g_kr",
             "g_sv", "w_sp", "b_sp", "w_pool", "pscale", "g_osgu", "g_opool"]
    head_spec = pl.BlockSpec((None, MLA_HEADS, TM, HEAD_PAD), lambda b, j: (b, 0, j, 0))
    head_shape = jax.ShapeDtypeStruct((B, MLA_HEADS, S, HEAD_PAD), BF16)
    return pl.pallas_call(
        _pre_kernel,
        grid=(B, nt),
        in_specs=[pl.BlockSpec((None, TM, D_MODEL), lambda b, j: (b, j, 0)),
                  pl.BlockSpec((None, TM, LANES), lambda b, j: (b, j, 0)),
                  pl.BlockSpec((None, TM, LANES), lambda b, j: (b, j, 0)),
                  _const_spec(ehead.shape)]
                 + [_layer_spec(p[n].shape, layer) for n in names],
        out_specs=[head_spec, head_spec,
                   pl.BlockSpec((None, MLA_HEADS, None, V_HEAD, TM), lambda b, j: (b, 0, j, 0, 0)),
                   pl.BlockSpec((None, TM, SGU_WIDTH + POOL_WIDTH), lambda b, j: (b, j, 0))],
        out_shape=[head_shape, head_shape,
                   jax.ShapeDtypeStruct((B, MLA_HEADS, nt, V_HEAD, TM), BF16),
                   jax.ShapeDtypeStruct((B, S, SGU_WIDTH + POOL_WIDTH), BF16)],
        scratch_shapes=[pltpu.VMEM((TM + MAX_WIN, POOL_WIDTH), F32)],
        compiler_params=pltpu.CompilerParams(dimension_semantics=("arbitrary", "arbitrary"),
                                             vmem_limit_bytes=VMEM_LIMIT),
        name="pre_mixer",
    )(x, ctab, stab, ehead, *[p[n] for n in names])


def _attn_kernel(q_ref, k_ref, v_ref, g_ref, o_ref, m_scr, l_scr, acc_scr):
    qi = pl.program_id(1)
    key = lax.broadcasted_iota(jnp.int32, (TK, TQ), 0)
    qry = lax.broadcasted_iota(jnp.int32, (TK, TQ), 1)
    causal = key <= qry
    m_scr[...] = jnp.full(m_scr.shape, NEG_BIG, F32)
    l_scr[...] = jnp.zeros(l_scr.shape, F32)
    acc_scr[...] = jnp.zeros(acc_scr.shape, F32)

    def scores(hh, jj):
        kblk = k_ref[hh, pl.ds(pl.multiple_of(jj * TK, TK), TK), :]
        return lax.dot_general(kblk, q_ref[hh], (((1,), (1,)), ((), ())),
                               preferred_element_type=F32)

    def softmax_update(hh, s):
        m_prev = m_scr[hh]
        m_next = jnp.maximum(m_prev, jnp.max(s, axis=0, keepdims=True))
        p = jnp.exp2(s - m_next)
        alpha = jnp.exp2(m_prev - m_next)
        l_scr[hh] = alpha * l_scr[hh] + jnp.sum(p, axis=0, keepdims=True)
        m_scr[hh] = m_next
        return p.astype(BF16), alpha

    def accumulate(hh, jj, p, alpha):
        acc_scr[hh] = acc_scr[hh] * alpha + jnp.dot(v_ref[hh, jj], p,
                                                    preferred_element_type=F32)

    def kv_step(jj, s_first, masked):
        s_next = s_first
        for hh in range(MLA_HEADS):
            s_cur = s_next
            if hh + 1 < MLA_HEADS:
                s_next = scores(hh + 1, jj)
            elif not masked:
                s_next = scores(0, jj + 1)
            if masked:
                s_cur = jnp.where(causal, s_cur, NEG_BIG)
            p, alpha = softmax_update(hh, s_cur)
            accumulate(hh, jj, p, alpha)
        return s_next

    s_diag = lax.fori_loop(0, qi, lambda jj, s: kv_step(jj, s, False), scores(0, 0))
    kv_step(qi, s_diag, True)
    a = jnp.concatenate([(acc_scr[hh] / l_scr[hh]).T for hh in range(MLA_HEADS)], axis=1)
    o_ref[...] = _rms(a, g_ref[...]).astype(BF16)


def _attention(q, k, v, g_out, layer):
    B, H, S, _ = q.shape
    k_spec = pl.BlockSpec((None, H, S, HEAD_PAD), lambda b, i: (b, 0, 0, 0))
    v_spec = pl.BlockSpec((None, H, S // TK, V_HEAD, TK), lambda b, i: (b, 0, 0, 0, 0))
    return pl.pallas_call(
        _attn_kernel,
        grid=(B, S // TQ),
        in_specs=[pl.BlockSpec((None, H, TQ, HEAD_PAD), lambda b, i: (b, 0, i, 0)),
                  k_spec, v_spec, _layer_spec(g_out.shape, layer)],
        out_specs=pl.BlockSpec((None, TQ, H * V_HEAD), lambda b, i: (b, i, 0)),
        out_shape=jax.ShapeDtypeStruct((B, S, H * V_HEAD), BF16),
        scratch_shapes=[pltpu.VMEM((H, 1, TQ), F32), pltpu.VMEM((H, 1, TQ), F32),
                        pltpu.VMEM((H, V_HEAD, TQ), F32)],
        compiler_params=pltpu.CompilerParams(dimension_semantics=("parallel", "arbitrary"),
                                             vmem_limit_bytes=VMEM_LIMIT),
        name="mla_attention",
    )(q, k, v, g_out)


def _post_kernel(x_ref, a_ref, mr_ref, wo_ref, gffn_ref, wg_ref, wu_ref, wd_ref, o_ref, h2_scr):
    n_a = a_ref.shape[1]
    x1 = (x_ref[...]
          + jnp.dot(a_ref[...], wo_ref[0:n_a, :], preferred_element_type=F32)
          + jnp.dot(mr_ref[...], wo_ref[n_a:, :], preferred_element_type=F32))
    h2_scr[...] = _rms(x1, gffn_ref[...]).astype(BF16)
    o_ref[...] = x1
    for cidx in range(N_FFN_CHUNKS):
        lo, hi = cidx * FFN_CHUNK, (cidx + 1) * FFN_CHUNK
        h2 = h2_scr[...]
        gate = jnp.dot(h2, wg_ref[:, lo:hi], preferred_element_type=F32)
        up = jnp.dot(h2, wu_ref[:, lo:hi], preferred_element_type=F32)
        act = (gate * (1.0 / (1.0 + jnp.exp(-gate))) * up).astype(BF16)
        o_ref[...] += jnp.dot(act, wd_ref[lo:hi, :], preferred_element_type=F32)


def _post_mixer(x2d, a2d, mr2d, p, layer):
    n = x2d.shape[0]
    names = ["w_out", "g_ffn", "w_gate", "w_up", "w_down"]
    return pl.pallas_call(
        _post_kernel,
        grid=(n // TM,),
        in_specs=[pl.BlockSpec((TM, D_MODEL), lambda i: (i, 0)),
                  pl.BlockSpec((TM, a2d.shape[1]), lambda i: (i, 0)),
                  pl.BlockSpec((TM, mr2d.shape[1]), lambda i: (i, 0))]
                 + [_layer_spec(p[nm].shape, layer, pipeline_mode=pl.Buffered(1)) for nm in names],
        out_specs=pl.BlockSpec((TM, D_MODEL), lambda i: (i, 0)),
        out_shape=jax.ShapeDtypeStruct((n, D_MODEL), F32),
        scratch_shapes=[pltpu.VMEM((TM, D_MODEL), BF16)],
        compiler_params=pltpu.CompilerParams(dimension_semantics=("parallel",),
                                             vmem_limit_bytes=VMEM_LIMIT),
        name="post_mixer",
    )(x2d, a2d, mr2d, *[p[nm] for nm in names])


def _prep_params(g_mix_norm, w_in, g_q_lat, w_q_up, g_kv_lat, w_kv_up, g_q_head, g_k_head,
                 g_sgu_v, w_spatial, b_spatial, w_pool, pool_scale, g_out_mla, g_out_sgu,
                 g_out_pool, w_out, g_ffn_norm, w_gate, w_up, w_down):
    depth = w_in.shape[0]
    o2 = Q_LORA + KV_LORA
    o3 = o2 + QK_ROPE
    zeros = functools.partial(jnp.zeros, dtype=F32)
    row = lambda g: g[:, None, :]

    kr = w_in[:, :, o2:o3]
    kr_blk = jnp.concatenate([zeros((depth, D_MODEL, QK_NOPE)), kr, kr[:, :, :ROPE_HALF],
                              zeros((depth, D_MODEL, LANES - ROPE_COPY_END))], axis=2)
    w_in_p = jnp.concatenate([w_in[:, :, :o2], kr_blk, w_in[:, :, o3:]], axis=2).astype(BF16)

    def swap_halves(t):
        lead = t.shape[:-1]
        return jnp.concatenate([zeros(lead + (QK_NOPE,)), t[..., QK_NOPE + ROPE_HALF:QK_HEAD],
                                t[..., QK_NOPE:QK_NOPE + ROPE_HALF], zeros(lead + (LANES - QK_HEAD,))], axis=-1)

    wq = w_q_up.reshape(depth, Q_LORA, MLA_HEADS, QK_HEAD)
    wq_p = jnp.concatenate([wq, zeros((depth, Q_LORA, MLA_HEADS, LANES - QK_HEAD))], axis=3)
    wq_p = wq_p.reshape(depth, Q_LORA, QK_WIDTH).astype(BF16)
    wqr_p = swap_halves(wq).reshape(depth, Q_LORA, QK_WIDTH).astype(BF16)

    wkv = w_kv_up.reshape(depth, KV_LORA, MLA_HEADS, QK_NOPE + V_HEAD)
    wkn = jnp.concatenate([wkv[..., :QK_NOPE], zeros((depth, KV_LORA, MLA_HEADS, HEAD_PAD - QK_NOPE))], axis=3)
    wkn_p = wkn.reshape(depth, KV_LORA, QK_WIDTH).astype(BF16)
    wvt_p = jnp.swapaxes(wkv[..., QK_NOPE:].reshape(depth, KV_LORA, MLA_HEADS * V_HEAD), 1, 2).astype(BF16)

    def pad_gain(g, copy):
        tail = g[:, QK_NOPE:QK_NOPE + ROPE_HALF] if copy else zeros((depth, ROPE_HALF))
        return row(jnp.concatenate([g, tail, zeros((depth, LANES - ROPE_COPY_END))], axis=1))

    w_sp = jnp.transpose(w_spatial, (0, 2, 1, 3)).reshape(depth, CHUNK, SGU_HEADS * CHUNK)
    b_sp = jnp.repeat(jnp.swapaxes(b_spatial, 1, 2), SGU_HEAD_DIM, axis=2)
    n_g = len(POOL_WINDOWS)
    eye = jnp.eye(n_g, dtype=F32)
    w_pool_bd = (eye[None, :, None, :, None] * w_pool[:, :, :, None, :]).reshape(
        depth, n_g * POOL_GROUP_DIM, n_g * POOL_GROUP_DIM).astype(BF16)

    return dict(
        g_mix=row(g_mix_norm), w_in=w_in_p, g_ql=row(g_q_lat), w_q=wq_p, w_qr=wqr_p,
        g_kvl=row(g_kv_lat), w_kn=wkn_p, w_vt=wvt_p,
        g_qh=pad_gain(g_q_head, False), g_qr=row(swap_halves(g_q_head)),
        g_kh=pad_gain(g_k_head, True), g_kr=row(swap_halves(g_k_head)),
        g_sv=row(g_sgu_v), w_sp=w_sp, b_sp=b_sp, w_pool=w_pool_bd,
        pscale=row(pool_scale), g_osgu=row(g_out_sgu), g_opool=row(g_out_pool), g_omla=row(g_out_mla),
        w_out=w_out.astype(BF16), g_ffn=row(g_ffn_norm),
        w_gate=w_gate.astype(BF16), w_up=w_up.astype(BF16), w_down=w_down.astype(BF16),
    )


def kernel(x, positions, g_mix_norm, w_in, g_q_lat, w_q_up, g_kv_lat, w_kv_up, g_q_head, g_k_head, g_sgu_v, w_spatial, b_spatial, w_pool, pool_scale, g_out_mla, g_out_sgu, g_out_pool, w_out, g_ffn_norm, w_gate, w_up, w_down):
    B, S, D = x.shape
    depth = w_in.shape[0]
    assert D == D_MODEL and S % TM == 0 and S % TQ == 0 and TQ == TK and TK == TM and TM % CHUNK == 0

    inv_freq = 1.0 / (ROPE_THETA ** (jnp.arange(ROPE_HALF, dtype=F32) / ROPE_HALF))
    invf_lane = jnp.concatenate([jnp.zeros((QK_NOPE,), F32), inv_freq, inv_freq,
                                 jnp.zeros((LANES - QK_HEAD,), F32)])[None, :]
    ctab, stab = _rope_tables(positions.astype(F32).reshape(B * S, 1), invf_lane)
    ctab = ctab.reshape(B, S, LANES)
    stab = stab.reshape(B, S, LANES)

    lane = jnp.arange(QK_WIDTH)
    ehead = ((lane[:, None] // HEAD_PAD == lane[None, :] // HEAD_PAD)
             & (lane[:, None] % HEAD_PAD < QK_HEAD)).astype(BF16)

    p = _prep_params(g_mix_norm, w_in, g_q_lat, w_q_up, g_kv_lat, w_kv_up, g_q_head, g_k_head,
                     g_sgu_v, w_spatial, b_spatial, w_pool, pool_scale, g_out_mla, g_out_sgu,
                     g_out_pool, w_out, g_ffn_norm, w_gate, w_up, w_down)
    for l in range(depth):
        q, k, v, mr = _pre_mixer(x, ctab, stab, ehead, p, l)
        a = _attention(q, k, v, p["g_omla"], l)
        x = _post_mixer(x.reshape(B * S, D), a.reshape(B * S, -1), mr.reshape(B * S, -1), p, l).reshape(B, S, D)
    return x
```

```python
import functools
import math

import jax
import jax.numpy as jnp
from jax import lax
from jax.experimental import pallas as pl
from jax.experimental.pallas import tpu as pltpu

D_MODEL = 1024
MLA_HEADS = 4
V_HEAD = 128
QK_NOPE = 64
QK_ROPE = 32
QK_HEAD = QK_NOPE + QK_ROPE
Q_LORA = 256
KV_LORA = 128
ROPE_THETA = 10000.0
SGU_WIDTH = 256
SGU_HEADS = 4
SGU_HEAD_DIM = 64
CHUNK = 128
POOL_WIDTH = 256
POOL_WINDOWS = (2, 4, 8, 16)
POOL_GROUP_DIM = 64
MAX_WIN = max(POOL_WINDOWS)
FFN_HIDDEN = 2816
EPS = 1e-6

LANES = 128
HEAD_PAD = LANES
V_AUG = V_HEAD + 16
QK_WIDTH = MLA_HEADS * HEAD_PAD
IN_PAD = Q_LORA + KV_LORA + LANES + 2 * SGU_WIDTH + POOL_WIDTH
ROPE_HALF = QK_ROPE // 2
ROPE_COPY_END = QK_HEAD + ROPE_HALF
Q_SCALE = math.log2(math.e) / math.sqrt(QK_HEAD)
NEG_BIG = -1e30

TM = 512
TQ = 512
TKS = 256
KV_UNROLL = TQ // TKS
LOOKAHEAD = 2
FFN_CHUNK = 256
N_FFN_CHUNKS = FFN_HIDDEN // FFN_CHUNK
VMEM_LIMIT = 56 * 1024 * 1024

F32 = jnp.float32
BF16 = jnp.bfloat16


def _rms(v, g):
    return v * lax.rsqrt(jnp.mean(v * v, axis=-1, keepdims=True) + EPS) * g


def _layer_spec(shape, layer, **kw):
    nd = len(shape) - 1
    return pl.BlockSpec((None,) + tuple(shape[1:]), lambda *_: (layer,) + (0,) * nd, **kw)


def _const_spec(shape):
    nd = len(shape)
    return pl.BlockSpec(shape, lambda *_: (0,) * nd)


def _rope_table_kernel(pos_ref, invf_ref, c_ref, s_ref):
    ang = pos_ref[...] * invf_ref[...]
    lane = lax.broadcasted_iota(jnp.int32, ang.shape, 1)
    cosv = jnp.cos(ang)
    sinv = jnp.sin(ang)
    c_ref[...] = jnp.where(lane < QK_NOPE, 1.0, jnp.where(lane < QK_HEAD, cosv, 0.0))
    first = (lane >= QK_NOPE) & (lane < QK_NOPE + ROPE_HALF)
    second = (lane >= QK_NOPE + ROPE_HALF) & (lane < QK_HEAD)
    s_ref[...] = jnp.where(first, -sinv, jnp.where(second, sinv, 0.0))


def _rope_tables(pos_f, invf_lane):
    n = pos_f.shape[0]
    return pl.pallas_call(
        _rope_table_kernel,
        grid=(n // TM,),
        in_specs=[pl.BlockSpec((TM, 1), lambda i: (i, 0)),
                  pl.BlockSpec((1, LANES), lambda i: (0, 0))],
        out_specs=[pl.BlockSpec((TM, LANES), lambda i: (i, 0)),
                   pl.BlockSpec((TM, LANES), lambda i: (i, 0))],
        out_shape=[jax.ShapeDtypeStruct((n, LANES), F32)] * 2,
        compiler_params=pltpu.CompilerParams(dimension_semantics=("parallel",)),
        name="rope_tables",
    )(pos_f, invf_lane)


def _pre_kernel(x_ref, c_ref, sn_ref, ehead_ref, gmix_ref, win_ref, gql_ref, wq_ref, wqr_ref, gkvl_ref,
                wkn_ref, wvt_ref, gqh_ref, gqr_ref, gkh_ref, gkr_ref, gsv_ref, wsp_ref, bsp_ref,
                wpool_ref, pscale_ref, gosgu_ref, gopool_ref,
                q_out, k_out, v_out, mr_out, pext_ref):
    j = pl.program_id(1)
    h = _rms(x_ref[...], gmix_ref[...]).astype(BF16)
    z = jnp.dot(h, win_ref[...], preferred_element_type=F32)
    o_kv = Q_LORA
    o_kr = o_kv + KV_LORA
    o_u = o_kr + LANES
    o_v = o_u + SGU_WIDTH
    o_p = o_v + SGU_WIDTH

    c = c_ref[...]
    sn = sn_ref[...]
    ehead = ehead_ref[...]

    qn_lat = _rms(z[:, 0:Q_LORA], gql_ref[...]).astype(BF16)
    q = jnp.dot(qn_lat, wq_ref[...], preferred_element_type=F32)
    q_sw = jnp.dot(qn_lat, wqr_ref[...], preferred_element_type=F32)
    rq = lax.rsqrt(jnp.dot((q * q).astype(BF16), ehead, preferred_element_type=F32) * (1.0 / QK_HEAD) + EPS)

    kvn = _rms(z[:, o_kv:o_kr], gkvl_ref[...]).astype(BF16)
    kn = jnp.dot(kvn, wkn_ref[...], preferred_element_type=F32)
    vt = lax.dot_general(wvt_ref[...], kvn, (((1,), (1,)), ((), ())), preferred_element_type=F32)
    kr = z[:, o_kr:o_u]
    kr_sw = pltpu.roll(kr, LANES - ROPE_HALF, 1)
    kf = jnp.concatenate([kn[:, hh * HEAD_PAD:(hh + 1) * HEAD_PAD] + kr for hh in range(MLA_HEADS)], axis=1)
    rk = lax.rsqrt(jnp.dot((kf * kf).astype(BF16), ehead, preferred_element_type=F32) * (1.0 / QK_HEAD) + EPS)
    ones_rows = (lax.broadcasted_iota(jnp.int32, (V_AUG - V_HEAD, TKS), 0) == 0).astype(BF16)
    for hh in range(MLA_HEADS):
        sl = slice(hh * HEAD_PAD, (hh + 1) * HEAD_PAD)
        r = rq[:, sl]
        q_out[hh] = (((q[:, sl] * r * gqh_ref[...]) * c + (q_sw[:, sl] * r * gqr_ref[...]) * sn)
                     * Q_SCALE).astype(BF16)
        r = rk[:, sl]
        k_out[hh] = ((kf[:, sl] * r * gkh_ref[...]) * c + (kr_sw * r * gkr_ref[...]) * sn).astype(BF16)
        for t in range(TM // TKS):
            v_out[hh, t, 0:V_HEAD, :] = vt[hh * V_HEAD:(hh + 1) * V_HEAD, t * TKS:(t + 1) * TKS].astype(BF16)
            v_out[hh, t, V_HEAD:V_AUG, :] = ones_rows

    u = z[:, o_u:o_v]
    vs = _rms(z[:, o_v:o_p], gsv_ref[...])
    row = lax.broadcasted_iota(jnp.int32, (CHUNK, SGU_HEADS * CHUNK), 0)
    col = lax.broadcasted_iota(jnp.int32, (CHUNK, SGU_HEADS * CHUNK), 1)
    wt = jnp.where((col & (CHUNK - 1)) <= row, wsp_ref[...], 0.0).astype(BF16)
    lane_head = lax.broadcasted_iota(jnp.int32, (1, SGU_WIDTH), 1) // SGU_HEAD_DIM
    for cc in range(TM // CHUNK):
        r0, r1 = cc * CHUNK, (cc + 1) * CHUNK
        vc = vs[r0:r1, :]
        vbd = jnp.concatenate([jnp.where(lane_head == hh, vc, 0.0) for hh in range(SGU_HEADS)],
                              axis=0).astype(BF16)
        zc = jnp.dot(wt, vbd, preferred_element_type=F32) + bsp_ref[...]
        gm = u[r0:r1, :] * zc
        mr_out[r0:r1, 0:SGU_WIDTH] = _rms(gm, gosgu_ref[...]).astype(BF16)

    pin = z[:, o_p:o_p + POOL_WIDTH]

    @pl.when(j == 0)
    def _():
        pext_ref[0:MAX_WIN, :] = jnp.zeros((MAX_WIN, POOL_WIDTH), F32)

    @pl.when(j != 0)
    def _():
        pext_ref[0:MAX_WIN, :] = pext_ref[TM:TM + MAX_WIN, :]

    pext_ref[MAX_WIN:MAX_WIN + TM, :] = pin

    def shifted(d, lo, hi):
        return pext_ref[MAX_WIN - d:MAX_WIN - d + TM, lo:hi]

    lane = lax.broadcasted_iota(jnp.int32, (1, LANES), 1)
    a = [shifted(d, 0, LANES) for d in range(4)]
    s2 = a[0] + a[1]
    s4 = s2 + (a[2] + a[3])
    b = [shifted(d, LANES, 2 * LANES) for d in range(16)]
    s8 = ((b[0] + b[1]) + (b[2] + b[3])) + ((b[4] + b[5]) + (b[6] + b[7]))
    s16 = s8 + (((b[8] + b[9]) + (b[10] + b[11])) + ((b[12] + b[13]) + (b[14] + b[15])))
    first_group = lane < POOL_GROUP_DIM
    wsum = jnp.concatenate([jnp.where(first_group, s2, s4), jnp.where(first_group, s8, s16)], axis=1)
    lane_p = lax.broadcasted_iota(jnp.int32, (1, POOL_WIDTH), 1) // POOL_GROUP_DIM
    win = jnp.where(lane_p == 0, float(POOL_WINDOWS[0]),
                    jnp.where(lane_p == 1, float(POOL_WINDOWS[1]),
                              jnp.where(lane_p == 2, float(POOL_WINDOWS[2]), float(POOL_WINDOWS[3]))))
    t1 = (lax.broadcasted_iota(jnp.int32, (TM, POOL_WIDTH), 0) + (j * TM + 1)).astype(F32)
    count = jnp.minimum(t1, win)
    m = (wsum / count - pin).astype(BF16)
    y = jnp.dot(m, wpool_ref[...], preferred_element_type=F32) * pscale_ref[...]
    mr_out[:, SGU_WIDTH:SGU_WIDTH + POOL_WIDTH] = _rms(y, gopool_ref[...]).astype(BF16)


def _pre_mixer(x, ctab, stab, ehead, p, layer):
    B, S, _ = x.shape
    nt = S // TM
    names = ["g_mix", "w_in", "g_ql", "w_q", "w_qr", "g_kvl", "w_kn", "w_vt", "g_qh", "g_qr", "g_kh", "g_kr",
             "g_sv", "w_sp", "b_sp", "w_pool", "pscale", "g_osgu", "g_opool"]
    head_spec = pl.BlockSpec((None, MLA_HEADS, TM, HEAD_PAD), lambda b, j: (b, 0, j, 0))
    head_shape = jax.ShapeDtypeStruct((B, MLA_HEADS, S, HEAD_PAD), BF16)
    return pl.pallas_call(
        _pre_kernel,
        grid=(B, nt),
        in_specs=[pl.BlockSpec((None, TM, D_MODEL), lambda b, j: (b, j, 0)),
                  pl.BlockSpec((None, TM, LANES), lambda b, j: (b, j, 0)),
                  pl.BlockSpec((None, TM, LANES), lambda b, j: (b, j, 0)),
                  _const_spec(ehead.shape)]
                 + [_layer_spec(p[n].shape, layer) for n in names],
        out_specs=[head_spec, head_spec,
                   pl.BlockSpec((None, MLA_HEADS, TM // TKS, V_AUG, TKS), lambda b, j: (b, 0, j, 0, 0)),
                   pl.BlockSpec((None, TM, SGU_WIDTH + POOL_WIDTH), lambda b, j: (b, j, 0))],
        out_shape=[head_shape, head_shape,
                   jax.ShapeDtypeStruct((B, MLA_HEADS, S // TKS, V_AUG, TKS), BF16),
                   jax.ShapeDtypeStruct((B, S, SGU_WIDTH + POOL_WIDTH), BF16)],
        scratch_shapes=[pltpu.VMEM((TM + MAX_WIN, POOL_WIDTH), F32)],
        compiler_params=pltpu.CompilerParams(dimension_semantics=("arbitrary", "arbitrary"),
                                             vmem_limit_bytes=VMEM_LIMIT),
        name="pre_mixer",
    )(x, ctab, stab, ehead, *[p[n] for n in names])


def _attn_kernel(q_ref, k_ref, v_ref, g_ref, o_ref, m_scr, acc_scr):
    qi = pl.program_id(1)
    key = lax.broadcasted_iota(jnp.int32, (TKS, TQ), 0)
    qry = lax.broadcasted_iota(jnp.int32, (TKS, TQ), 1)
    m_scr[...] = jnp.full(m_scr.shape, NEG_BIG, F32)
    acc_scr[...] = jnp.zeros(acc_scr.shape, F32)

    def scores(hh, tile):
        kblk = k_ref[hh, pl.ds(pl.multiple_of(tile * TKS, TKS), TKS), :]
        return lax.dot_general(kblk, q_ref[hh], (((1,), (1,)), ((), ())),
                               preferred_element_type=F32)

    def softmax_update(hh, s):
        m_prev = m_scr[hh]
        m_next = jnp.maximum(m_prev, jnp.max(s, axis=0, keepdims=True))
        p = jnp.exp2(s - m_next)
        alpha = jnp.exp2(m_prev - m_next)
        m_scr[hh] = m_next
        return p.astype(BF16), alpha

    def accumulate(hh, tile, p, alpha):
        acc_scr[hh] = acc_scr[hh] * alpha + jnp.dot(v_ref[hh, tile], p,
                                                    preferred_element_type=F32)

    chains = [(hh, t) for t in range(KV_UNROLL) for hh in range(MLA_HEADS)]

    def kv_group(grp, carried, masked):
        pending = list(carried)
        for i, (hh, t) in enumerate(chains):
            ahead = i + LOOKAHEAD
            if ahead < len(chains):
                pending.append(scores(chains[ahead][0], grp * KV_UNROLL + chains[ahead][1]))
            elif not masked:
                h2, t2 = chains[ahead - len(chains)]
                pending.append(scores(h2, (grp + 1) * KV_UNROLL + t2))
            s = pending.pop(0)
            if masked:
                s = jnp.where(key + t * TKS <= qry, s, NEG_BIG)
            p, alpha = softmax_update(hh, s)
            accumulate(hh, grp * KV_UNROLL + t, p, alpha)
        return tuple(pending)

    first = tuple(scores(hh, t) for hh, t in chains[:LOOKAHEAD])
    diag = lax.fori_loop(0, qi, lambda grp, c: kv_group(grp, c, False), first)
    kv_group(qi, diag, True)
    a = jnp.concatenate([(acc_scr[hh, 0:V_HEAD, :] / acc_scr[hh, V_HEAD:V_HEAD + 1, :]).T
                         for hh in range(MLA_HEADS)], axis=1)
    o_ref[...] = _rms(a, g_ref[...]).astype(BF16)


def _attention(q, k, v, g_out, layer):
    B, H, S, _ = q.shape
    k_spec = pl.BlockSpec((None, H, S, HEAD_PAD), lambda b, i: (b, 0, 0, 0))
    v_spec = pl.BlockSpec((None, H, S // TKS, V_AUG, TKS), lambda b, i: (b, 0, 0, 0, 0))
    return pl.pallas_call(
        _attn_kernel,
        grid=(B, S // TQ),
        in_specs=[pl.BlockSpec((None, H, TQ, HEAD_PAD), lambda b, i: (b, 0, i, 0)),
                  k_spec, v_spec, _layer_spec(g_out.shape, layer)],
        out_specs=pl.BlockSpec((None, TQ, H * V_HEAD), lambda b, i: (b, i, 0)),
        out_shape=jax.ShapeDtypeStruct((B, S, H * V_HEAD), BF16),
        scratch_shapes=[pltpu.VMEM((H, 1, TQ), F32), pltpu.VMEM((H, V_AUG, TQ), F32)],
        compiler_params=pltpu.CompilerParams(dimension_semantics=("parallel", "arbitrary"),
                                             vmem_limit_bytes=VMEM_LIMIT),
        name="mla_attention",
    )(q, k, v, g_out)


def _post_kernel(x_ref, a_ref, mr_ref, wo_ref, gffn_ref, wg_ref, wu_ref, wd_ref, o_ref, h2_scr):
    n_a = a_ref.shape[1]
    x1 = (x_ref[...]
          + jnp.dot(a_ref[...], wo_ref[0:n_a, :], preferred_element_type=F32)
          + jnp.dot(mr_ref[...], wo_ref[n_a:, :], preferred_element_type=F32))
    h2_scr[...] = _rms(x1, gffn_ref[...]).astype(BF16)
    o_ref[...] = x1
    for cidx in range(N_FFN_CHUNKS):
        lo, hi = cidx * FFN_CHUNK, (cidx + 1) * FFN_CHUNK
        h2 = h2_scr[...]
        gate = jnp.dot(h2, wg_ref[:, lo:hi], preferred_element_type=F32)
        up = jnp.dot(h2, wu_ref[:, lo:hi], preferred_element_type=F32)
        act = (gate * (1.0 / (1.0 + jnp.exp(-gate))) * up).astype(BF16)
        o_ref[...] += jnp.dot(act, wd_ref[lo:hi, :], preferred_element_type=F32)


def _post_mixer(x2d, a2d, mr2d, p, layer):
    n = x2d.shape[0]
    names = ["w_out", "g_ffn", "w_gate", "w_up", "w_down"]
    return pl.pallas_call(
        _post_kernel,
        grid=(n // TM,),
        in_specs=[pl.BlockSpec((TM, D_MODEL), lambda i: (i, 0)),
                  pl.BlockSpec((TM, a2d.shape[1]), lambda i: (i, 0)),
                  pl.BlockSpec((TM, mr2d.shape[1]), lambda i: (i, 0))]
                 + [_layer_spec(p[nm].shape, layer, pipeline_mode=pl.Buffered(1)) for nm in names],
        out_specs=pl.BlockSpec((TM, D_MODEL), lambda i: (i, 0)),
        out_shape=jax.ShapeDtypeStruct((n, D_MODEL), F32),
        scratch_shapes=[pltpu.VMEM((TM, D_MODEL), BF16)],
        compiler_params=pltpu.CompilerParams(dimension_semantics=("parallel",),
                                             vmem_limit_bytes=VMEM_LIMIT),
        name="post_mixer",
    )(x2d, a2d, mr2d, *[p[nm] for nm in names])


def _prep_params(g_mix_norm, w_in, g_q_lat, w_q_up, g_kv_lat, w_kv_up, g_q_head, g_k_head,
                 g_sgu_v, w_spatial, b_spatial, w_pool, pool_scale, g_out_mla, g_out_sgu,
                 g_out_pool, w_out, g_ffn_norm, w_gate, w_up, w_down):
    depth = w_in.shape[0]
    o2 = Q_LORA + KV_LORA
    o3 = o2 + QK_ROPE
    zeros = functools.partial(jnp.zeros, dtype=F32)
    row = lambda g: g[:, None, :]

    kr = w_in[:, :, o2:o3]
    kr_blk = jnp.concatenate([zeros((depth, D_MODEL, QK_NOPE)), kr, kr[:, :, :ROPE_HALF],
                              zeros((depth, D_MODEL, LANES - ROPE_COPY_END))], axis=2)
    w_in_p = jnp.concatenate([w_in[:, :, :o2], kr_blk, w_in[:, :, o3:]], axis=2).astype(BF16)

    def swap_halves(t):
        lead = t.shape[:-1]
        return jnp.concatenate([zeros(lead + (QK_NOPE,)), t[..., QK_NOPE + ROPE_HALF:QK_HEAD],
                                t[..., QK_NOPE:QK_NOPE + ROPE_HALF], zeros(lead + (LANES - QK_HEAD,))], axis=-1)

    wq = w_q_up.reshape(depth, Q_LORA, MLA_HEADS, QK_HEAD)
    wq_p = jnp.concatenate([wq, zeros((depth, Q_LORA, MLA_HEADS, LANES - QK_HEAD))], axis=3)
    wq_p = wq_p.reshape(depth, Q_LORA, QK_WIDTH).astype(BF16)
    wqr_p = swap_halves(wq).reshape(depth, Q_LORA, QK_WIDTH).astype(BF16)

    wkv = w_kv_up.reshape(depth, KV_LORA, MLA_HEADS, QK_NOPE + V_HEAD)
    wkn = jnp.concatenate([wkv[..., :QK_NOPE], zeros((depth, KV_LORA, MLA_HEADS, HEAD_PAD - QK_NOPE))], axis=3)
    wkn_p = wkn.reshape(depth, KV_LORA, QK_WIDTH).astype(BF16)
    wvt_p = jnp.swapaxes(wkv[..., QK_NOPE:].reshape(depth, KV_LORA, MLA_HEADS * V_HEAD), 1, 2).astype(BF16)

    def pad_gain(g, copy):
        tail = g[:, QK_NOPE:QK_NOPE + ROPE_HALF] if copy else zeros((depth, ROPE_HALF))
        return row(jnp.concatenate([g, tail, zeros((depth, LANES - ROPE_COPY_END))], axis=1))

    w_sp = jnp.transpose(w_spatial, (0, 2, 1, 3)).reshape(depth, CHUNK, SGU_HEADS * CHUNK)
    b_sp = jnp.repeat(jnp.swapaxes(b_spatial, 1, 2), SGU_HEAD_DIM, axis=2)
    n_g = len(POOL_WINDOWS)
    eye = jnp.eye(n_g, dtype=F32)
    w_pool_bd = (eye[None, :, None, :, None] * w_pool[:, :, :, None, :]).reshape(
        depth, n_g * POOL_GROUP_DIM, n_g * POOL_GROUP_DIM).astype(BF16)

    return dict(
        g_mix=row(g_mix_norm), w_in=w_in_p, g_ql=row(g_q_lat), w_q=wq_p, w_qr=wqr_p,
        g_kvl=row(g_kv_lat), w_kn=wkn_p, w_vt=wvt_p,
        g_qh=pad_gain(g_q_head, False), g_qr=row(swap_halves(g_q_head)),
        g_kh=pad_gain(g_k_head, True), g_kr=row(swap_halves(g_k_head)),
        g_sv=row(g_sgu_v), w_sp=w_sp, b_sp=b_sp, w_pool=w_pool_bd,
        pscale=row(pool_scale), g_osgu=row(g_out_sgu), g_opool=row(g_out_pool), g_omla=row(g_out_mla),
        w_out=w_out.astype(BF16), g_ffn=row(g_ffn_norm),
        w_gate=w_gate.astype(BF16), w_up=w_up.astype(BF16), w_down=w_down.astype(BF16),
    )


def kernel(x, positions, g_mix_norm, w_in, g_q_lat, w_q_up, g_kv_lat, w_kv_up, g_q_head, g_k_head, g_sgu_v, w_spatial, b_spatial, w_pool, pool_scale, g_out_mla, g_out_sgu, g_out_pool, w_out, g_ffn_norm, w_gate, w_up, w_down):
    B, S, D = x.shape
    depth = w_in.shape[0]
    assert D == D_MODEL and S % TM == 0 and S % TQ == 0 and TM % TKS == 0 and TM % CHUNK == 0

    inv_freq = 1.0 / (ROPE_THETA ** (jnp.arange(ROPE_HALF, dtype=F32) / ROPE_HALF))
    invf_lane = jnp.concatenate([jnp.zeros((QK_NOPE,), F32), inv_freq, inv_freq,
                                 jnp.zeros((LANES - QK_HEAD,), F32)])[None, :]
    ctab, stab = _rope_tables(positions.astype(F32).reshape(B * S, 1), invf_lane)
    ctab = ctab.reshape(B, S, LANES)
    stab = stab.reshape(B, S, LANES)

    lane = jnp.arange(QK_WIDTH)
    ehead = ((lane[:, None] // HEAD_PAD == lane[None, :] // HEAD_PAD)
             & (lane[:, None] % HEAD_PAD < QK_HEAD)).astype(BF16)

    p = _prep_params(g_mix_norm, w_in, g_q_lat, w_q_up, g_kv_lat, w_kv_up, g_q_head, g_k_head,
                     g_sgu_v, w_spatial, b_spatial, w_pool, pool_scale, g_out_mla, g_out_sgu,
                     g_out_pool, w_out, g_ffn_norm, w_gate, w_up, w_down)
    for l in range(depth):
        q, k, v, mr = _pre_mixer(x, ctab, stab, ehead, p, l)
        a = _attention(q, k, v, p["g_omla"], l)
        x = _post_mixer(x.reshape(B * S, D), a.reshape(B * S, -1), mr.reshape(B * S, -1), p, l).reshape(B, S, D)
    return x
```

```python
import functools
import math

import jax
import jax.numpy as jnp
from jax import lax
from jax.experimental import pallas as pl
from jax.experimental.pallas import tpu as pltpu

D_MODEL = 1024
MLA_HEADS = 4
V_HEAD = 128
QK_NOPE = 64
QK_ROPE = 32
QK_HEAD = QK_NOPE + QK_ROPE
Q_LORA = 256
KV_LORA = 128
ROPE_THETA = 10000.0
SGU_WIDTH = 256
SGU_HEADS = 4
SGU_HEAD_DIM = 64
CHUNK = 128
POOL_WIDTH = 256
POOL_WINDOWS = (2, 4, 8, 16)
POOL_GROUP_DIM = 64
MAX_WIN = max(POOL_WINDOWS)
FFN_HIDDEN = 2816
EPS = 1e-6

LANES = 128
HEAD_PAD = LANES
V_AUG = V_HEAD + 16
QK_WIDTH = MLA_HEADS * HEAD_PAD
IN_PAD = Q_LORA + KV_LORA + LANES + 2 * SGU_WIDTH + POOL_WIDTH
ROPE_HALF = QK_ROPE // 2
ROPE_COPY_END = QK_HEAD + ROPE_HALF
Q_SCALE = math.log2(math.e) / math.sqrt(QK_HEAD)
NEG_BIG = -1e30

TM = 512
TQ = 512
TKS = 256
DIAG_SUB = TQ // TKS
LOOKAHEAD = 2
FFN_CHUNK = 256
N_FFN_CHUNKS = FFN_HIDDEN // FFN_CHUNK
VMEM_LIMIT = 56 * 1024 * 1024

F32 = jnp.float32
BF16 = jnp.bfloat16


def _rms(v, g):
    return v * lax.rsqrt(jnp.mean(v * v, axis=-1, keepdims=True) + EPS) * g


def _layer_spec(shape, layer, **kw):
    nd = len(shape) - 1
    return pl.BlockSpec((None,) + tuple(shape[1:]), lambda *_: (layer,) + (0,) * nd, **kw)


def _const_spec(shape):
    nd = len(shape)
    return pl.BlockSpec(shape, lambda *_: (0,) * nd)


def _rope_table_kernel(pos_ref, invf_ref, c_ref, s_ref):
    ang = pos_ref[...] * invf_ref[...]
    lane = lax.broadcasted_iota(jnp.int32, ang.shape, 1)
    cosv = jnp.cos(ang)
    sinv = jnp.sin(ang)
    c_ref[...] = jnp.where(lane < QK_NOPE, 1.0, jnp.where(lane < QK_HEAD, cosv, 0.0))
    first = (lane >= QK_NOPE) & (lane < QK_NOPE + ROPE_HALF)
    second = (lane >= QK_NOPE + ROPE_HALF) & (lane < QK_HEAD)
    s_ref[...] = jnp.where(first, -sinv, jnp.where(second, sinv, 0.0))


def _rope_tables(pos_f, invf_lane):
    n = pos_f.shape[0]
    return pl.pallas_call(
        _rope_table_kernel,
        grid=(n // TM,),
        in_specs=[pl.BlockSpec((TM, 1), lambda i: (i, 0)),
                  pl.BlockSpec((1, LANES), lambda i: (0, 0))],
        out_specs=[pl.BlockSpec((TM, LANES), lambda i: (i, 0)),
                   pl.BlockSpec((TM, LANES), lambda i: (i, 0))],
        out_shape=[jax.ShapeDtypeStruct((n, LANES), F32)] * 2,
        compiler_params=pltpu.CompilerParams(dimension_semantics=("parallel",)),
        name="rope_tables",
    )(pos_f, invf_lane)


def _pre_kernel(x_ref, c_ref, sn_ref, ehead_ref, gmix_ref, win_ref, gql_ref, wq_ref, wqr_ref, gkvl_ref,
                wkn_ref, wvt_ref, gqh_ref, gqr_ref, gkh_ref, gkr_ref, gsv_ref, wsp_ref, bsp_ref,
                wpool_ref, pscale_ref, gosgu_ref, gopool_ref,
                q_out, k_out, v_out, mr_out, pext_ref):
    j = pl.program_id(1)
    h = _rms(x_ref[...], gmix_ref[...]).astype(BF16)
    z = jnp.dot(h, win_ref[...], preferred_element_type=F32)
    o_kv = Q_LORA
    o_kr = o_kv + KV_LORA
    o_u = o_kr + LANES
    o_v = o_u + SGU_WIDTH
    o_p = o_v + SGU_WIDTH

    c = c_ref[...]
    sn = sn_ref[...]
    ehead = ehead_ref[...]

    qn_lat = _rms(z[:, 0:Q_LORA], gql_ref[...]).astype(BF16)
    q = jnp.dot(qn_lat, wq_ref[...], preferred_element_type=F32)
    q_sw = jnp.dot(qn_lat, wqr_ref[...], preferred_element_type=F32)
    rq = lax.rsqrt(jnp.dot((q * q).astype(BF16), ehead, preferred_element_type=F32) * (1.0 / QK_HEAD) + EPS)

    kvn = _rms(z[:, o_kv:o_kr], gkvl_ref[...]).astype(BF16)
    kn = jnp.dot(kvn, wkn_ref[...], preferred_element_type=F32)
    vt = lax.dot_general(wvt_ref[...], kvn, (((1,), (1,)), ((), ())), preferred_element_type=F32)
    kr = z[:, o_kr:o_u]
    kr_sw = pltpu.roll(kr, LANES - ROPE_HALF, 1)
    kf = jnp.concatenate([kn[:, hh * HEAD_PAD:(hh + 1) * HEAD_PAD] + kr for hh in range(MLA_HEADS)], axis=1)
    rk = lax.rsqrt(jnp.dot((kf * kf).astype(BF16), ehead, preferred_element_type=F32) * (1.0 / QK_HEAD) + EPS)
    ones_rows = (lax.broadcasted_iota(jnp.int32, (V_AUG - V_HEAD, TKS), 0) == 0).astype(BF16)
    for hh in range(MLA_HEADS):
        sl = slice(hh * HEAD_PAD, (hh + 1) * HEAD_PAD)
        r = rq[:, sl]
        q_out[hh] = (((q[:, sl] * r * gqh_ref[...]) * c + (q_sw[:, sl] * r * gqr_ref[...]) * sn)
                     * Q_SCALE).astype(BF16)
        r = rk[:, sl]
        k_out[hh] = ((kf[:, sl] * r * gkh_ref[...]) * c + (kr_sw * r * gkr_ref[...]) * sn).astype(BF16)
        for t in range(TM // TKS):
            v_out[hh, t, 0:V_HEAD, :] = vt[hh * V_HEAD:(hh + 1) * V_HEAD, t * TKS:(t + 1) * TKS].astype(BF16)
            v_out[hh, t, V_HEAD:V_AUG, :] = ones_rows

    u = z[:, o_u:o_v]
    vs = _rms(z[:, o_v:o_p], gsv_ref[...])
    row = lax.broadcasted_iota(jnp.int32, (CHUNK, SGU_HEADS * CHUNK), 0)
    col = lax.broadcasted_iota(jnp.int32, (CHUNK, SGU_HEADS * CHUNK), 1)
    wt = jnp.where((col & (CHUNK - 1)) <= row, wsp_ref[...], 0.0).astype(BF16)
    lane_head = lax.broadcasted_iota(jnp.int32, (1, SGU_WIDTH), 1) // SGU_HEAD_DIM
    for cc in range(TM // CHUNK):
        r0, r1 = cc * CHUNK, (cc + 1) * CHUNK
        vc = vs[r0:r1, :]
        vbd = jnp.concatenate([jnp.where(lane_head == hh, vc, 0.0) for hh in range(SGU_HEADS)],
                              axis=0).astype(BF16)
        zc = jnp.dot(wt, vbd, preferred_element_type=F32) + bsp_ref[...]
        gm = u[r0:r1, :] * zc
        mr_out[r0:r1, 0:SGU_WIDTH] = _rms(gm, gosgu_ref[...]).astype(BF16)

    pin = z[:, o_p:o_p + POOL_WIDTH]

    @pl.when(j == 0)
    def _():
        pext_ref[0:MAX_WIN, :] = jnp.zeros((MAX_WIN, POOL_WIDTH), F32)

    @pl.when(j != 0)
    def _():
        pext_ref[0:MAX_WIN, :] = pext_ref[TM:TM + MAX_WIN, :]

    pext_ref[MAX_WIN:MAX_WIN + TM, :] = pin

    def shifted(d, lo, hi):
        return pext_ref[MAX_WIN - d:MAX_WIN - d + TM, lo:hi]

    lane = lax.broadcasted_iota(jnp.int32, (1, LANES), 1)
    a = [shifted(d, 0, LANES) for d in range(4)]
    s2 = a[0] + a[1]
    s4 = s2 + (a[2] + a[3])
    b = [shifted(d, LANES, 2 * LANES) for d in range(16)]
    s8 = ((b[0] + b[1]) + (b[2] + b[3])) + ((b[4] + b[5]) + (b[6] + b[7]))
    s16 = s8 + (((b[8] + b[9]) + (b[10] + b[11])) + ((b[12] + b[13]) + (b[14] + b[15])))
    first_group = lane < POOL_GROUP_DIM
    wsum = jnp.concatenate([jnp.where(first_group, s2, s4), jnp.where(first_group, s8, s16)], axis=1)
    lane_p = lax.broadcasted_iota(jnp.int32, (1, POOL_WIDTH), 1) // POOL_GROUP_DIM
    win = jnp.where(lane_p == 0, float(POOL_WINDOWS[0]),
                    jnp.where(lane_p == 1, float(POOL_WINDOWS[1]),
                              jnp.where(lane_p == 2, float(POOL_WINDOWS[2]), float(POOL_WINDOWS[3]))))
    t1 = (lax.broadcasted_iota(jnp.int32, (TM, POOL_WIDTH), 0) + (j * TM + 1)).astype(F32)
    count = jnp.minimum(t1, win)
    m = (wsum / count - pin).astype(BF16)
    y = jnp.dot(m, wpool_ref[...], preferred_element_type=F32) * pscale_ref[...]
    mr_out[:, SGU_WIDTH:SGU_WIDTH + POOL_WIDTH] = _rms(y, gopool_ref[...]).astype(BF16)


def _pre_mixer(x, ctab, stab, ehead, p, layer):
    B, S, _ = x.shape
    nt = S // TM
    names = ["g_mix", "w_in", "g_ql", "w_q", "w_qr", "g_kvl", "w_kn", "w_vt", "g_qh", "g_qr", "g_kh", "g_kr",
             "g_sv", "w_sp", "b_sp", "w_pool", "pscale", "g_osgu", "g_opool"]
    head_spec = pl.BlockSpec((None, MLA_HEADS, TM, HEAD_PAD), lambda b, j: (b, 0, j, 0))
    head_shape = jax.ShapeDtypeStruct((B, MLA_HEADS, S, HEAD_PAD), BF16)
    return pl.pallas_call(
        _pre_kernel,
        grid=(B, nt),
        in_specs=[pl.BlockSpec((None, TM, D_MODEL), lambda b, j: (b, j, 0)),
                  pl.BlockSpec((None, TM, LANES), lambda b, j: (b, j, 0)),
                  pl.BlockSpec((None, TM, LANES), lambda b, j: (b, j, 0)),
                  _const_spec(ehead.shape)]
                 + [_layer_spec(p[n].shape, layer) for n in names],
        out_specs=[head_spec, head_spec,
                   pl.BlockSpec((None, MLA_HEADS, TM // TKS, V_AUG, TKS), lambda b, j: (b, 0, j, 0, 0)),
                   pl.BlockSpec((None, TM, SGU_WIDTH + POOL_WIDTH), lambda b, j: (b, j, 0))],
        out_shape=[head_shape, head_shape,
                   jax.ShapeDtypeStruct((B, MLA_HEADS, S // TKS, V_AUG, TKS), BF16),
                   jax.ShapeDtypeStruct((B, S, SGU_WIDTH + POOL_WIDTH), BF16)],
        scratch_shapes=[pltpu.VMEM((TM + MAX_WIN, POOL_WIDTH), F32)],
        compiler_params=pltpu.CompilerParams(dimension_semantics=("arbitrary", "arbitrary"),
                                             vmem_limit_bytes=VMEM_LIMIT),
        name="pre_mixer",
    )(x, ctab, stab, ehead, *[p[n] for n in names])


def _attn_kernel(q_ref, k_ref, v_ref, bias_ref, g_ref, o_ref, m_scr, acc_scr):
    qi = pl.program_id(1)
    m_scr[...] = jnp.full(m_scr.shape, NEG_BIG, F32)
    acc_scr[...] = jnp.zeros(acc_scr.shape, F32)

    def scores(hh, tile, q_lo=0):
        kblk = k_ref[hh, pl.ds(pl.multiple_of(tile * TKS, TKS), TKS), :]
        return lax.dot_general(kblk, q_ref[hh, q_lo:, :], (((1,), (1,)), ((), ())),
                               preferred_element_type=F32)

    def softmax_update(hh, s, q_lo=0):
        m_prev = m_scr[hh, :, q_lo:]
        m_next = jnp.maximum(m_prev, jnp.max(s, axis=0, keepdims=True))
        p = jnp.exp2(s - m_next)
        alpha = jnp.exp2(m_prev - m_next)
        m_scr[hh, :, q_lo:] = m_next
        return p.astype(BF16), alpha

    def accumulate(hh, tile, p, alpha, q_lo=0):
        acc_scr[hh, :, q_lo:] = acc_scr[hh, :, q_lo:] * alpha + jnp.dot(
            v_ref[hh, tile], p, preferred_element_type=F32)

    def chain_at(base, i):
        return i % MLA_HEADS, base + i // MLA_HEADS

    def kv_group(base, n_sub, carried, masked):
        pending = list(carried)
        q_lo = lambda i: (i // MLA_HEADS) * TKS if masked else 0
        for i in range(n_sub * MLA_HEADS):
            ahead = i + LOOKAHEAD
            if not masked or ahead < n_sub * MLA_HEADS:
                pending.append(scores(*chain_at(base, ahead), q_lo(ahead)))
            hh, tile = chain_at(base, i)
            s = pending.pop(0)
            if masked:
                s = jnp.where(bias_ref[i // MLA_HEADS, :, q_lo(i):] < 0.0, NEG_BIG, s)
            p, alpha = softmax_update(hh, s, q_lo(i))
            accumulate(hh, tile, p, alpha, q_lo(i))
        return tuple(pending)

    carried = tuple(scores(*chain_at(0, i)) for i in range(LOOKAHEAD))
    n_double = lax.shift_right_logical(qi, 1)
    carried = lax.fori_loop(0, n_double,
                            lambda g, c: kv_group(g * (2 * DIAG_SUB), 2 * DIAG_SUB, c, False), carried)
    carried = lax.fori_loop(0, qi & 1,
                            lambda g, c: kv_group((n_double * 2 + g) * DIAG_SUB, DIAG_SUB, c, False), carried)
    kv_group(qi * DIAG_SUB, DIAG_SUB, carried, True)
    a = jnp.concatenate([(acc_scr[hh, 0:V_HEAD, :] / acc_scr[hh, V_HEAD:V_HEAD + 1, :]).T
                         for hh in range(MLA_HEADS)], axis=1)
    o_ref[...] = _rms(a, g_ref[...]).astype(BF16)


def _attention(q, k, v, g_out, layer):
    B, H, S, _ = q.shape
    key = jnp.arange(TQ)[:, None]
    bias = jnp.where(key <= jnp.arange(TQ)[None, :], 0.0, NEG_BIG).astype(F32).reshape(DIAG_SUB, TKS, TQ)
    k_spec = pl.BlockSpec((None, H, S, HEAD_PAD), lambda b, i: (b, 0, 0, 0))
    v_spec = pl.BlockSpec((None, H, S // TKS, V_AUG, TKS), lambda b, i: (b, 0, 0, 0, 0))
    return pl.pallas_call(
        _attn_kernel,
        grid=(B, S // TQ),
        in_specs=[pl.BlockSpec((None, H, TQ, HEAD_PAD), lambda b, i: (b, 0, i, 0)),
                  k_spec, v_spec, _const_spec(bias.shape), _layer_spec(g_out.shape, layer)],
        out_specs=pl.BlockSpec((None, TQ, H * V_HEAD), lambda b, i: (b, i, 0)),
        out_shape=jax.ShapeDtypeStruct((B, S, H * V_HEAD), BF16),
        scratch_shapes=[pltpu.VMEM((H, 1, TQ), F32), pltpu.VMEM((H, V_AUG, TQ), F32)],
        compiler_params=pltpu.CompilerParams(dimension_semantics=("parallel", "arbitrary"),
                                             vmem_limit_bytes=VMEM_LIMIT),
        name="mla_attention",
    )(q, k, v, bias, g_out)


def _post_kernel(x_ref, a_ref, mr_ref, wo_ref, gffn_ref, wg_ref, wu_ref, wd_ref, o_ref, h2_scr):
    n_a = a_ref.shape[1]
    x1 = (x_ref[...]
          + jnp.dot(a_ref[...], wo_ref[0:n_a, :], preferred_element_type=F32)
          + jnp.dot(mr_ref[...], wo_ref[n_a:, :], preferred_element_type=F32))
    h2_scr[...] = _rms(x1, gffn_ref[...]).astype(BF16)
    o_ref[...] = x1
    for cidx in range(N_FFN_CHUNKS):
        lo, hi = cidx * FFN_CHUNK, (cidx + 1) * FFN_CHUNK
        h2 = h2_scr[...]
        gate = jnp.dot(h2, wg_ref[:, lo:hi], preferred_element_type=F32)
        up = jnp.dot(h2, wu_ref[:, lo:hi], preferred_element_type=F32)
        act = (gate * (1.0 / (1.0 + jnp.exp(-gate))) * up).astype(BF16)
        o_ref[...] += jnp.dot(act, wd_ref[lo:hi, :], preferred_element_type=F32)


def _post_mixer(x2d, a2d, mr2d, p, layer):
    n = x2d.shape[0]
    names = ["w_out", "g_ffn", "w_gate", "w_up", "w_down"]
    return pl.pallas_call(
        _post_kernel,
        grid=(n // TM,),
        in_specs=[pl.BlockSpec((TM, D_MODEL), lambda i: (i, 0)),
                  pl.BlockSpec((TM, a2d.shape[1]), lambda i: (i, 0)),
                  pl.BlockSpec((TM, mr2d.shape[1]), lambda i: (i, 0))]
                 + [_layer_spec(p[nm].shape, layer, pipeline_mode=pl.Buffered(1)) for nm in names],
        out_specs=pl.BlockSpec((TM, D_MODEL), lambda i: (i, 0)),
        out_shape=jax.ShapeDtypeStruct((n, D_MODEL), F32),
        scratch_shapes=[pltpu.VMEM((TM, D_MODEL), BF16)],
        compiler_params=pltpu.CompilerParams(dimension_semantics=("parallel",),
                                             vmem_limit_bytes=VMEM_LIMIT),
        name="post_mixer",
    )(x2d, a2d, mr2d, *[p[nm] for nm in names])


def _prep_params(g_mix_norm, w_in, g_q_lat, w_q_up, g_kv_lat, w_kv_up, g_q_head, g_k_head,
                 g_sgu_v, w_spatial, b_spatial, w_pool, pool_scale, g_out_mla, g_out_sgu,
                 g_out_pool, w_out, g_ffn_norm, w_gate, w_up, w_down):
    depth = w_in.shape[0]
    o2 = Q_LORA + KV_LORA
    o3 = o2 + QK_ROPE
    zeros = functools.partial(jnp.zeros, dtype=F32)
    row = lambda g: g[:, None, :]

    kr = w_in[:, :, o2:o3]
    kr_blk = jnp.concatenate([zeros((depth, D_MODEL, QK_NOPE)), kr, kr[:, :, :ROPE_HALF],
                              zeros((depth, D_MODEL, LANES - ROPE_COPY_END))], axis=2)
    w_in_p = jnp.concatenate([w_in[:, :, :o2], kr_blk, w_in[:, :, o3:]], axis=2).astype(BF16)

    def swap_halves(t):
        lead = t.shape[:-1]
        return jnp.concatenate([zeros(lead + (QK_NOPE,)), t[..., QK_NOPE + ROPE_HALF:QK_HEAD],
                                t[..., QK_NOPE:QK_NOPE + ROPE_HALF], zeros(lead + (LANES - QK_HEAD,))], axis=-1)

    wq = w_q_up.reshape(depth, Q_LORA, MLA_HEADS, QK_HEAD)
    wq_p = jnp.concatenate([wq, zeros((depth, Q_LORA, MLA_HEADS, LANES - QK_HEAD))], axis=3)
    wq_p = wq_p.reshape(depth, Q_LORA, QK_WIDTH).astype(BF16)
    wqr_p = swap_halves(wq).reshape(depth, Q_LORA, QK_WIDTH).astype(BF16)

    wkv = w_kv_up.reshape(depth, KV_LORA, MLA_HEADS, QK_NOPE + V_HEAD)
    wkn = jnp.concatenate([wkv[..., :QK_NOPE], zeros((depth, KV_LORA, MLA_HEADS, HEAD_PAD - QK_NOPE))], axis=3)
    wkn_p = wkn.reshape(depth, KV_LORA, QK_WIDTH).astype(BF16)
    wvt_p = jnp.swapaxes(wkv[..., QK_NOPE:].reshape(depth, KV_LORA, MLA_HEADS * V_HEAD), 1, 2).astype(BF16)

    def pad_gain(g, copy):
        tail = g[:, QK_NOPE:QK_NOPE + ROPE_HALF] if copy else zeros((depth, ROPE_HALF))
        return row(jnp.concatenate([g, tail, zeros((depth, LANES - ROPE_COPY_END))], axis=1))

    w_sp = jnp.transpose(w_spatial, (0, 2, 1, 3)).reshape(depth, CHUNK, SGU_HEADS * CHUNK)
    b_sp = jnp.repeat(jnp.swapaxes(b_spatial, 1, 2), SGU_HEAD_DIM, axis=2)
    n_g = len(POOL_WINDOWS)
    eye = jnp.eye(n_g, dtype=F32)
    w_pool_bd = (eye[None, :, None, :, None] * w_pool[:, :, :, None, :]).reshape(
        depth, n_g * POOL_GROUP_DIM, n_g * POOL_GROUP_DIM).astype(BF16)

    return dict(
        g_mix=row(g_mix_norm), w_in=w_in_p, g_ql=row(g_q_lat), w_q=wq_p, w_qr=wqr_p,
        g_kvl=row(g_kv_lat), w_kn=wkn_p, w_vt=wvt_p,
        g_qh=pad_gain(g_q_head, False), g_qr=row(swap_halves(g_q_head)),
        g_kh=pad_gain(g_k_head, True), g_kr=row(swap_halves(g_k_head)),
        g_sv=row(g_sgu_v), w_sp=w_sp, b_sp=b_sp, w_pool=w_pool_bd,
        pscale=row(pool_scale), g_osgu=row(g_out_sgu), g_opool=row(g_out_pool), g_omla=row(g_out_mla),
        w_out=w_out.astype(BF16), g_ffn=row(g_ffn_norm),
        w_gate=w_gate.astype(BF16), w_up=w_up.astype(BF16), w_down=w_down.astype(BF16),
    )


def kernel(x, positions, g_mix_norm, w_in, g_q_lat, w_q_up, g_kv_lat, w_kv_up, g_q_head, g_k_head, g_sgu_v, w_spatial, b_spatial, w_pool, pool_scale, g_out_mla, g_out_sgu, g_out_pool, w_out, g_ffn_norm, w_gate, w_up, w_down):
    B, S, D = x.shape
    depth = w_in.shape[0]
    assert D == D_MODEL and S % TM == 0 and S % TQ == 0 and TM % TKS == 0 and TM % CHUNK == 0

    inv_freq = 1.0 / (ROPE_THETA ** (jnp.arange(ROPE_HALF, dtype=F32) / ROPE_HALF))
    invf_lane = jnp.concatenate([jnp.zeros((QK_NOPE,), F32), inv_freq, inv_freq,
                                 jnp.zeros((LANES - QK_HEAD,), F32)])[None, :]
    ctab, stab = _rope_tables(positions.astype(F32).reshape(B * S, 1), invf_lane)
    ctab = ctab.reshape(B, S, LANES)
    stab = stab.reshape(B, S, LANES)

    lane = jnp.arange(QK_WIDTH)
    ehead = ((lane[:, None] // HEAD_PAD == lane[None, :] // HEAD_PAD)
             & (lane[:, None] % HEAD_PAD < QK_HEAD)).astype(BF16)

    p = _prep_params(g_mix_norm, w_in, g_q_lat, w_q_up, g_kv_lat, w_kv_up, g_q_head, g_k_head,
                     g_sgu_v, w_spatial, b_spatial, w_pool, pool_scale, g_out_mla, g_out_sgu,
                     g_out_pool, w_out, g_ffn_norm, w_gate, w_up, w_down)
    for l in range(depth):
        q, k, v, mr = _pre_mixer(x, ctab, stab, ehead, p, l)
        a = _attention(q, k, v, p["g_omla"], l)
        x = _post_mixer(x.reshape(B * S, D), a.reshape(B * S, -1), mr.reshape(B * S, -1), p, l).reshape(B, S, D)
    return x
```

```python
import functools
import math

import jax
import jax.numpy as jnp
from jax import lax
from jax.experimental import pallas as pl
from jax.experimental.pallas import tpu as pltpu

D_MODEL = 1024
MLA_HEADS = 4
V_HEAD = 128
QK_NOPE = 64
QK_ROPE = 32
QK_HEAD = QK_NOPE + QK_ROPE
Q_LORA = 256
KV_LORA = 128
ROPE_THETA = 10000.0
SGU_WIDTH = 256
SGU_HEADS = 4
SGU_HEAD_DIM = 64
CHUNK = 128
POOL_WIDTH = 256
POOL_WINDOWS = (2, 4, 8, 16)
POOL_GROUP_DIM = 64
MAX_WIN = max(POOL_WINDOWS)
FFN_HIDDEN = 2816
EPS = 1e-6

LANES = 128
HEAD_PAD = LANES
V_AUG = V_HEAD + 16
QK_WIDTH = MLA_HEADS * HEAD_PAD
IN_PAD = Q_LORA + KV_LORA + LANES + 2 * SGU_WIDTH + POOL_WIDTH
ROPE_HALF = QK_ROPE // 2
ROPE_COPY_END = QK_HEAD + ROPE_HALF
Q_SCALE = math.log2(math.e) / math.sqrt(QK_HEAD)
NEG_BIG = -1e30

TM = 512
TQ = 512
TKS = 256
DIAG_SUB = TQ // TKS
LOOKAHEAD = 2
FFN_CHUNK = 256
N_FFN_CHUNKS = FFN_HIDDEN // FFN_CHUNK
VMEM_LIMIT = 56 * 1024 * 1024

F32 = jnp.float32
BF16 = jnp.bfloat16


def _rms(v, g):
    return v * lax.rsqrt(jnp.mean(v * v, axis=-1, keepdims=True) + EPS) * g


def _layer_spec(shape, layer, **kw):
    nd = len(shape) - 1
    return pl.BlockSpec((None,) + tuple(shape[1:]), lambda *_: (layer,) + (0,) * nd, **kw)


def _const_spec(shape):
    nd = len(shape)
    return pl.BlockSpec(shape, lambda *_: (0,) * nd)


def _rope_table_kernel(pos_ref, invf_ref, c_ref, s_ref):
    ang = invf_ref[...] * pos_ref[...]
    cosv = jnp.cos(ang)
    sinv = jnp.sin(ang)
    ones = jnp.ones((QK_NOPE, TM), F32)
    zeros = lambda n: jnp.zeros((n, TM), F32)
    c_ref[...] = jnp.concatenate([ones, cosv, cosv, zeros(LANES - QK_HEAD)], axis=0).T
    s_ref[...] = jnp.concatenate([zeros(QK_NOPE), -sinv, sinv, zeros(LANES - QK_HEAD)], axis=0).T


def _rope_tables(pos_f, inv_freq):
    nt = pos_f.shape[0]
    return pl.pallas_call(
        _rope_table_kernel,
        grid=(nt,),
        in_specs=[pl.BlockSpec((None, 1, TM), lambda i: (i, 0, 0)),
                  pl.BlockSpec((ROPE_HALF, 1), lambda i: (0, 0))],
        out_specs=[pl.BlockSpec((TM, LANES), lambda i: (i, 0)),
                   pl.BlockSpec((TM, LANES), lambda i: (i, 0))],
        out_shape=[jax.ShapeDtypeStruct((nt * TM, LANES), F32)] * 2,
        compiler_params=pltpu.CompilerParams(dimension_semantics=("parallel",)),
        name="rope_tables",
    )(pos_f, inv_freq)


def _pre_kernel(x_ref, c_ref, sn_ref, ehead_ref, gmix_ref, win_ref, gql_ref, wq_ref, wqr_ref, gkvl_ref,
                wkn_ref, wvt_ref, gqh_ref, gqr_ref, gkh_ref, gkr_ref, gsv_ref, wsp_ref, bsp_ref,
                wpool_ref, pscale_ref, gosgu_ref, gopool_ref,
                q_out, k_out, v_out, mr_out, pext_ref):
    j = pl.program_id(1)
    h = _rms(x_ref[...], gmix_ref[...]).astype(BF16)
    z = jnp.dot(h, win_ref[...], preferred_element_type=F32)
    o_kv = Q_LORA
    o_kr = o_kv + KV_LORA
    o_u = o_kr + LANES
    o_v = o_u + SGU_WIDTH
    o_p = o_v + SGU_WIDTH

    c = c_ref[...]
    sn = sn_ref[...]
    ehead = ehead_ref[...]

    qn_lat = _rms(z[:, 0:Q_LORA], gql_ref[...]).astype(BF16)
    q = jnp.dot(qn_lat, wq_ref[...], preferred_element_type=F32)
    q_sw = jnp.dot(qn_lat, wqr_ref[...], preferred_element_type=F32)
    rq = lax.rsqrt(jnp.dot((q * q).astype(BF16), ehead, preferred_element_type=F32) * (1.0 / QK_HEAD) + EPS)

    kvn = _rms(z[:, o_kv:o_kr], gkvl_ref[...]).astype(BF16)
    kn = jnp.dot(kvn, wkn_ref[...], preferred_element_type=F32)
    vt = lax.dot_general(wvt_ref[...], kvn, (((1,), (1,)), ((), ())), preferred_element_type=F32)
    kr = z[:, o_kr:o_u]
    kr_sw = pltpu.roll(kr, LANES - ROPE_HALF, 1)
    kf = jnp.concatenate([kn[:, hh * HEAD_PAD:(hh + 1) * HEAD_PAD] + kr for hh in range(MLA_HEADS)], axis=1)
    rk = lax.rsqrt(jnp.dot((kf * kf).astype(BF16), ehead, preferred_element_type=F32) * (1.0 / QK_HEAD) + EPS)
    ones_rows = (lax.broadcasted_iota(jnp.int32, (V_AUG - V_HEAD, TKS), 0) == 0).astype(BF16)
    for hh in range(MLA_HEADS):
        sl = slice(hh * HEAD_PAD, (hh + 1) * HEAD_PAD)
        r = rq[:, sl]
        q_out[hh] = (((q[:, sl] * r * gqh_ref[...]) * c + (q_sw[:, sl] * r * gqr_ref[...]) * sn)
                     * Q_SCALE).astype(BF16)
        r = rk[:, sl]
        k_out[hh] = ((kf[:, sl] * r * gkh_ref[...]) * c + (kr_sw * r * gkr_ref[...]) * sn).astype(BF16)
        for t in range(TM // TKS):
            v_out[hh, t, 0:V_HEAD, :] = vt[hh * V_HEAD:(hh + 1) * V_HEAD, t * TKS:(t + 1) * TKS].astype(BF16)
            v_out[hh, t, V_HEAD:V_AUG, :] = ones_rows

    u = z[:, o_u:o_v]
    vs = _rms(z[:, o_v:o_p], gsv_ref[...])
    row = lax.broadcasted_iota(jnp.int32, (CHUNK, SGU_HEADS * CHUNK), 0)
    col = lax.broadcasted_iota(jnp.int32, (CHUNK, SGU_HEADS * CHUNK), 1)
    wt = jnp.where((col & (CHUNK - 1)) <= row, wsp_ref[...], 0.0).astype(BF16)
    lane_head = lax.broadcasted_iota(jnp.int32, (1, SGU_WIDTH), 1) // SGU_HEAD_DIM
    for cc in range(TM // CHUNK):
        r0, r1 = cc * CHUNK, (cc + 1) * CHUNK
        vc = vs[r0:r1, :]
        vbd = jnp.concatenate([jnp.where(lane_head == hh, vc, 0.0) for hh in range(SGU_HEADS)],
                              axis=0).astype(BF16)
        zc = jnp.dot(wt, vbd, preferred_element_type=F32) + bsp_ref[...]
        gm = u[r0:r1, :] * zc
        mr_out[r0:r1, 0:SGU_WIDTH] = _rms(gm, gosgu_ref[...]).astype(BF16)

    pin = z[:, o_p:o_p + POOL_WIDTH]

    @pl.when(j == 0)
    def _():
        pext_ref[0:MAX_WIN, :] = jnp.zeros((MAX_WIN, POOL_WIDTH), F32)

    @pl.when(j != 0)
    def _():
        pext_ref[0:MAX_WIN, :] = pext_ref[TM:TM + MAX_WIN, :]

    pext_ref[MAX_WIN:MAX_WIN + TM, :] = pin

    def shifted(d, lo, hi):
        return pext_ref[MAX_WIN - d:MAX_WIN - d + TM, lo:hi]

    lane = lax.broadcasted_iota(jnp.int32, (1, LANES), 1)
    a = [shifted(d, 0, LANES) for d in range(4)]
    s2 = a[0] + a[1]
    s4 = s2 + (a[2] + a[3])
    b = [shifted(d, LANES, 2 * LANES) for d in range(16)]
    s8 = ((b[0] + b[1]) + (b[2] + b[3])) + ((b[4] + b[5]) + (b[6] + b[7]))
    s16 = s8 + (((b[8] + b[9]) + (b[10] + b[11])) + ((b[12] + b[13]) + (b[14] + b[15])))
    first_group = lane < POOL_GROUP_DIM
    wsum = jnp.concatenate([jnp.where(first_group, s2, s4), jnp.where(first_group, s8, s16)], axis=1)
    lane_p = lax.broadcasted_iota(jnp.int32, (1, POOL_WIDTH), 1) // POOL_GROUP_DIM
    win = jnp.where(lane_p == 0, float(POOL_WINDOWS[0]),
                    jnp.where(lane_p == 1, float(POOL_WINDOWS[1]),
                              jnp.where(lane_p == 2, float(POOL_WINDOWS[2]), float(POOL_WINDOWS[3]))))
    t1 = (lax.broadcasted_iota(jnp.int32, (TM, POOL_WIDTH), 0) + (j * TM + 1)).astype(F32)
    count = jnp.minimum(t1, win)
    m = (wsum / count - pin).astype(BF16)
    y = jnp.dot(m, wpool_ref[...], preferred_element_type=F32) * pscale_ref[...]
    mr_out[:, SGU_WIDTH:SGU_WIDTH + POOL_WIDTH] = _rms(y, gopool_ref[...]).astype(BF16)


def _pre_mixer(x, ctab, stab, ehead, p, layer):
    B, S, _ = x.shape
    nt = S // TM
    names = ["g_mix", "w_in", "g_ql", "w_q", "w_qr", "g_kvl", "w_kn", "w_vt", "g_qh", "g_qr", "g_kh", "g_kr",
             "g_sv", "w_sp", "b_sp", "w_pool", "pscale", "g_osgu", "g_opool"]
    head_spec = pl.BlockSpec((None, MLA_HEADS, TM, HEAD_PAD), lambda b, j: (b, 0, j, 0))
    head_shape = jax.ShapeDtypeStruct((B, MLA_HEADS, S, HEAD_PAD), BF16)
    return pl.pallas_call(
        _pre_kernel,
        grid=(B, nt),
        in_specs=[pl.BlockSpec((None, TM, D_MODEL), lambda b, j: (b, j, 0)),
                  pl.BlockSpec((None, TM, LANES), lambda b, j: (b, j, 0)),
                  pl.BlockSpec((None, TM, LANES), lambda b, j: (b, j, 0)),
                  _const_spec(ehead.shape)]
                 + [_layer_spec(p[n].shape, layer) for n in names],
        out_specs=[head_spec, head_spec,
                   pl.BlockSpec((None, MLA_HEADS, TM // TKS, V_AUG, TKS), lambda b, j: (b, 0, j, 0, 0)),
                   pl.BlockSpec((None, TM, SGU_WIDTH + POOL_WIDTH), lambda b, j: (b, j, 0))],
        out_shape=[head_shape, head_shape,
                   jax.ShapeDtypeStruct((B, MLA_HEADS, S // TKS, V_AUG, TKS), BF16),
                   jax.ShapeDtypeStruct((B, S, SGU_WIDTH + POOL_WIDTH), BF16)],
        scratch_shapes=[pltpu.VMEM((TM + MAX_WIN, POOL_WIDTH), F32)],
        compiler_params=pltpu.CompilerParams(dimension_semantics=("arbitrary", "arbitrary"),
                                             vmem_limit_bytes=VMEM_LIMIT),
        name="pre_mixer",
    )(x, ctab, stab, ehead, *[p[n] for n in names])


def _attn_kernel(q_ref, k_ref, v_ref, bias_ref, g_ref, o_ref, m_scr, acc_scr):
    qi = pl.program_id(1)
    m_scr[...] = jnp.full(m_scr.shape, NEG_BIG, F32)
    acc_scr[...] = jnp.zeros(acc_scr.shape, F32)

    def scores(hh, tile, q_lo=0):
        kblk = k_ref[hh, pl.ds(pl.multiple_of(tile * TKS, TKS), TKS), :]
        return lax.dot_general(kblk, q_ref[hh, q_lo:, :], (((1,), (1,)), ((), ())),
                               preferred_element_type=F32)

    def softmax_update(hh, s, q_lo=0):
        m_prev = m_scr[hh, :, q_lo:]
        m_next = jnp.maximum(m_prev, jnp.max(s, axis=0, keepdims=True))
        p = jnp.exp2(s - m_next)
        alpha = jnp.exp2(m_prev - m_next)
        m_scr[hh, :, q_lo:] = m_next
        return p.astype(BF16), alpha

    def accumulate(hh, tile, p, alpha, q_lo=0):
        acc_scr[hh, :, q_lo:] = acc_scr[hh, :, q_lo:] * alpha + jnp.dot(
            v_ref[hh, tile], p, preferred_element_type=F32)

    def chain_at(base, i):
        return i % MLA_HEADS, base + i // MLA_HEADS

    def kv_group(base, n_sub, carried, masked):
        pending = list(carried)
        q_lo = lambda i: (i // MLA_HEADS) * TKS if masked else 0
        for i in range(n_sub * MLA_HEADS):
            ahead = i + LOOKAHEAD
            if not masked or ahead < n_sub * MLA_HEADS:
                pending.append(scores(*chain_at(base, ahead), q_lo(ahead)))
            hh, tile = chain_at(base, i)
            s = pending.pop(0)
            if masked:
                s = jnp.where(bias_ref[i // MLA_HEADS, :, q_lo(i):] < 0.0, NEG_BIG, s)
            p, alpha = softmax_update(hh, s, q_lo(i))
            accumulate(hh, tile, p, alpha, q_lo(i))
        return tuple(pending)

    carried = tuple(scores(*chain_at(0, i)) for i in range(LOOKAHEAD))
    n_double = lax.shift_right_logical(qi, 1)
    carried = lax.fori_loop(0, n_double,
                            lambda g, c: kv_group(g * (2 * DIAG_SUB), 2 * DIAG_SUB, c, False), carried)
    carried = lax.fori_loop(0, qi & 1,
                            lambda g, c: kv_group((n_double * 2 + g) * DIAG_SUB, DIAG_SUB, c, False), carried)
    kv_group(qi * DIAG_SUB, DIAG_SUB, carried, True)
    a = jnp.concatenate([(acc_scr[hh, 0:V_HEAD, :] / acc_scr[hh, V_HEAD:V_HEAD + 1, :]).T
                         for hh in range(MLA_HEADS)], axis=1)
    o_ref[...] = _rms(a, g_ref[...]).astype(BF16)


def _attention(q, k, v, g_out, layer):
    B, H, S, _ = q.shape
    key = jnp.arange(TQ)[:, None]
    bias = jnp.where(key <= jnp.arange(TQ)[None, :], 0.0, NEG_BIG).astype(F32).reshape(DIAG_SUB, TKS, TQ)
    k_spec = pl.BlockSpec((None, H, S, HEAD_PAD), lambda b, i: (b, 0, 0, 0))
    v_spec = pl.BlockSpec((None, H, S // TKS, V_AUG, TKS), lambda b, i: (b, 0, 0, 0, 0))
    return pl.pallas_call(
        _attn_kernel,
        grid=(B, S // TQ),
        in_specs=[pl.BlockSpec((None, H, TQ, HEAD_PAD), lambda b, i: (b, 0, i, 0)),
                  k_spec, v_spec, _const_spec(bias.shape), _layer_spec(g_out.shape, layer)],
        out_specs=pl.BlockSpec((None, TQ, H * V_HEAD), lambda b, i: (b, i, 0)),
        out_shape=jax.ShapeDtypeStruct((B, S, H * V_HEAD), BF16),
        scratch_shapes=[pltpu.VMEM((H, 1, TQ), F32), pltpu.VMEM((H, V_AUG, TQ), F32)],
        compiler_params=pltpu.CompilerParams(dimension_semantics=("parallel", "arbitrary"),
                                             vmem_limit_bytes=VMEM_LIMIT),
        name="mla_attention",
    )(q, k, v, bias, g_out)


def _post_kernel(x_ref, a_ref, mr_ref, wo_ref, gffn_ref, wg_ref, wu_ref, wd_ref, o_ref, h2_scr):
    n_a = a_ref.shape[1]
    x1 = (x_ref[...]
          + jnp.dot(a_ref[...], wo_ref[0:n_a, :], preferred_element_type=F32)
          + jnp.dot(mr_ref[...], wo_ref[n_a:, :], preferred_element_type=F32))
    h2_scr[...] = _rms(x1, gffn_ref[...]).astype(BF16)
    o_ref[...] = x1
    for cidx in range(N_FFN_CHUNKS):
        lo, hi = cidx * FFN_CHUNK, (cidx + 1) * FFN_CHUNK
        h2 = h2_scr[...]
        gate = jnp.dot(h2, wg_ref[:, lo:hi], preferred_element_type=F32)
        up = jnp.dot(h2, wu_ref[:, lo:hi], preferred_element_type=F32)
        act = (gate * (1.0 / (1.0 + jnp.exp(-gate))) * up).astype(BF16)
        o_ref[...] += jnp.dot(act, wd_ref[lo:hi, :], preferred_element_type=F32)


def _post_mixer(x2d, a2d, mr2d, p, layer):
    n = x2d.shape[0]
    names = ["w_out", "g_ffn", "w_gate", "w_up", "w_down"]
    return pl.pallas_call(
        _post_kernel,
        grid=(n // TM,),
        in_specs=[pl.BlockSpec((TM, D_MODEL), lambda i: (i, 0)),
                  pl.BlockSpec((TM, a2d.shape[1]), lambda i: (i, 0)),
                  pl.BlockSpec((TM, mr2d.shape[1]), lambda i: (i, 0))]
                 + [_layer_spec(p[nm].shape, layer, pipeline_mode=pl.Buffered(1)) for nm in names],
        out_specs=pl.BlockSpec((TM, D_MODEL), lambda i: (i, 0)),
        out_shape=jax.ShapeDtypeStruct((n, D_MODEL), F32),
        scratch_shapes=[pltpu.VMEM((TM, D_MODEL), BF16)],
        compiler_params=pltpu.CompilerParams(dimension_semantics=("parallel",),
                                             vmem_limit_bytes=VMEM_LIMIT),
        name="post_mixer",
    )(x2d, a2d, mr2d, *[p[nm] for nm in names])


def _prep_params(g_mix_norm, w_in, g_q_lat, w_q_up, g_kv_lat, w_kv_up, g_q_head, g_k_head,
                 g_sgu_v, w_spatial, b_spatial, w_pool, pool_scale, g_out_mla, g_out_sgu,
                 g_out_pool, w_out, g_ffn_norm, w_gate, w_up, w_down):
    depth = w_in.shape[0]
    o2 = Q_LORA + KV_LORA
    o3 = o2 + QK_ROPE
    zeros = functools.partial(jnp.zeros, dtype=F32)
    row = lambda g: g[:, None, :]

    kr = w_in[:, :, o2:o3]
    kr_blk = jnp.concatenate([zeros((depth, D_MODEL, QK_NOPE)), kr, kr[:, :, :ROPE_HALF],
                              zeros((depth, D_MODEL, LANES - ROPE_COPY_END))], axis=2)
    w_in_p = jnp.concatenate([w_in[:, :, :o2], kr_blk, w_in[:, :, o3:]], axis=2).astype(BF16)

    def swap_halves(t):
        lead = t.shape[:-1]
        return jnp.concatenate([zeros(lead + (QK_NOPE,)), t[..., QK_NOPE + ROPE_HALF:QK_HEAD],
                                t[..., QK_NOPE:QK_NOPE + ROPE_HALF], zeros(lead + (LANES - QK_HEAD,))], axis=-1)

    wq = w_q_up.reshape(depth, Q_LORA, MLA_HEADS, QK_HEAD)
    wq_p = jnp.concatenate([wq, zeros((depth, Q_LORA, MLA_HEADS, LANES - QK_HEAD))], axis=3)
    wq_p = wq_p.reshape(depth, Q_LORA, QK_WIDTH).astype(BF16)
    wqr_p = swap_halves(wq).reshape(depth, Q_LORA, QK_WIDTH).astype(BF16)

    wkv = w_kv_up.reshape(depth, KV_LORA, MLA_HEADS, QK_NOPE + V_HEAD)
    wkn = jnp.concatenate([wkv[..., :QK_NOPE], zeros((depth, KV_LORA, MLA_HEADS, HEAD_PAD - QK_NOPE))], axis=3)
    wkn_p = wkn.reshape(depth, KV_LORA, QK_WIDTH).astype(BF16)
    wvt_p = jnp.swapaxes(wkv[..., QK_NOPE:].reshape(depth, KV_LORA, MLA_HEADS * V_HEAD), 1, 2).astype(BF16)

    def pad_gain(g, copy):
        tail = g[:, QK_NOPE:QK_NOPE + ROPE_HALF] if copy else zeros((depth, ROPE_HALF))
        return row(jnp.concatenate([g, tail, zeros((depth, LANES - ROPE_COPY_END))], axis=1))

    w_sp = jnp.transpose(w_spatial, (0, 2, 1, 3)).reshape(depth, CHUNK, SGU_HEADS * CHUNK)
    b_sp = jnp.repeat(jnp.swapaxes(b_spatial, 1, 2), SGU_HEAD_DIM, axis=2)
    n_g = len(POOL_WINDOWS)
    eye = jnp.eye(n_g, dtype=F32)
    w_pool_bd = (eye[None, :, None, :, None] * w_pool[:, :, :, None, :]).reshape(
        depth, n_g * POOL_GROUP_DIM, n_g * POOL_GROUP_DIM).astype(BF16)

    return dict(
        g_mix=row(g_mix_norm), w_in=w_in_p, g_ql=row(g_q_lat), w_q=wq_p, w_qr=wqr_p,
        g_kvl=row(g_kv_lat), w_kn=wkn_p, w_vt=wvt_p,
        g_qh=pad_gain(g_q_head, False), g_qr=row(swap_halves(g_q_head)),
        g_kh=pad_gain(g_k_head, True), g_kr=row(swap_halves(g_k_head)),
        g_sv=row(g_sgu_v), w_sp=w_sp, b_sp=b_sp, w_pool=w_pool_bd,
        pscale=row(pool_scale), g_osgu=row(g_out_sgu), g_opool=row(g_out_pool), g_omla=row(g_out_mla),
        w_out=w_out.astype(BF16), g_ffn=row(g_ffn_norm),
        w_gate=w_gate.astype(BF16), w_up=w_up.astype(BF16), w_down=w_down.astype(BF16),
    )


def kernel(x, positions, g_mix_norm, w_in, g_q_lat, w_q_up, g_kv_lat, w_kv_up, g_q_head, g_k_head, g_sgu_v, w_spatial, b_spatial, w_pool, pool_scale, g_out_mla, g_out_sgu, g_out_pool, w_out, g_ffn_norm, w_gate, w_up, w_down):
    B, S, D = x.shape
    depth = w_in.shape[0]
    assert D == D_MODEL and S % TM == 0 and S % TQ == 0 and TM % TKS == 0 and TM % CHUNK == 0

    inv_freq = 1.0 / (ROPE_THETA ** (jnp.arange(ROPE_HALF, dtype=F32) / ROPE_HALF))
    ctab, stab = _rope_tables(positions.astype(F32).reshape(B * S // TM, 1, TM), inv_freq[:, None])
    ctab = ctab.reshape(B, S, LANES)
    stab = stab.reshape(B, S, LANES)

    lane = jnp.arange(QK_WIDTH)
    ehead = ((lane[:, None] // HEAD_PAD == lane[None, :] // HEAD_PAD)
             & (lane[:, None] % HEAD_PAD < QK_HEAD)).astype(BF16)

    p = _prep_params(g_mix_norm, w_in, g_q_lat, w_q_up, g_kv_lat, w_kv_up, g_q_head, g_k_head,
                     g_sgu_v, w_spatial, b_spatial, w_pool, pool_scale, g_out_mla, g_out_sgu,
                     g_out_pool, w_out, g_ffn_norm, w_gate, w_up, w_down)
    for l in range(depth):
        q, k, v, mr = _pre_mixer(x, ctab, stab, ehead, p, l)
        a = _attention(q, k, v, p["g_omla"], l)
        x = _post_mixer(x.reshape(B * S, D), a.reshape(B * S, -1), mr.reshape(B * S, -1), p, l).reshape(B, S, D)
    return x
```

```python
import functools
import math

import jax
import jax.numpy as jnp
from jax import lax
from jax.experimental import pallas as pl
from jax.experimental.pallas import tpu as pltpu

D_MODEL = 1024
MLA_HEADS = 4
V_HEAD = 128
QK_NOPE = 64
QK_ROPE = 32
QK_HEAD = QK_NOPE + QK_ROPE
Q_LORA = 256
KV_LORA = 128
ROPE_THETA = 10000.0
SGU_WIDTH = 256
SGU_HEADS = 4
SGU_HEAD_DIM = 64
CHUNK = 128
POOL_WIDTH = 256
POOL_WINDOWS = (2, 4, 8, 16)
POOL_GROUP_DIM = 64
MAX_WIN = max(POOL_WINDOWS)
FFN_HIDDEN = 2816
EPS = 1e-6

LANES = 128
HEAD_PAD = LANES
V_AUG = V_HEAD + 16
QK_WIDTH = MLA_HEADS * HEAD_PAD
IN_PAD = Q_LORA + KV_LORA + LANES + 2 * SGU_WIDTH + POOL_WIDTH
ROPE_HALF = QK_ROPE // 2
ROPE_COPY_END = QK_HEAD + ROPE_HALF
Q_SCALE = math.log2(math.e) / math.sqrt(QK_HEAD)
NEG_BIG = -1e30

TM = 512
TQ = 512
TKS = 256
DIAG_SUB = TQ // TKS
LOOKAHEAD = 2
RING = 4
FFN_CHUNK = 256
N_FFN_CHUNKS = FFN_HIDDEN // FFN_CHUNK
VMEM_LIMIT = 56 * 1024 * 1024

F32 = jnp.float32
BF16 = jnp.bfloat16


def _rms(v, g):
    return v * lax.rsqrt(jnp.mean(v * v, axis=-1, keepdims=True) + EPS) * g


def _layer_spec(shape, layer, **kw):
    nd = len(shape) - 1
    return pl.BlockSpec((None,) + tuple(shape[1:]), lambda *_: (layer,) + (0,) * nd, **kw)


def _const_spec(shape):
    nd = len(shape)
    return pl.BlockSpec(shape, lambda *_: (0,) * nd)


def _rope_table_kernel(pos_ref, invf_ref, c_ref, s_ref):
    ang = invf_ref[...] * pos_ref[...]
    cosv = jnp.cos(ang)
    sinv = jnp.sin(ang)
    ones = jnp.ones((QK_NOPE, TM), F32)
    zeros = lambda n: jnp.zeros((n, TM), F32)
    c_ref[...] = jnp.concatenate([ones, cosv, cosv, zeros(LANES - QK_HEAD)], axis=0).T
    s_ref[...] = jnp.concatenate([zeros(QK_NOPE), -sinv, sinv, zeros(LANES - QK_HEAD)], axis=0).T


def _rope_tables(pos_f, inv_freq):
    nt = pos_f.shape[0]
    return pl.pallas_call(
        _rope_table_kernel,
        grid=(nt,),
        in_specs=[pl.BlockSpec((None, 1, TM), lambda i: (i, 0, 0)),
                  pl.BlockSpec((ROPE_HALF, 1), lambda i: (0, 0))],
        out_specs=[pl.BlockSpec((TM, LANES), lambda i: (i, 0)),
                   pl.BlockSpec((TM, LANES), lambda i: (i, 0))],
        out_shape=[jax.ShapeDtypeStruct((nt * TM, LANES), F32)] * 2,
        compiler_params=pltpu.CompilerParams(dimension_semantics=("parallel",)),
        name="rope_tables",
    )(pos_f, inv_freq)


def _pre_kernel(x_ref, c_ref, sn_ref, ehead_ref, gmix_ref, win_ref, gql_ref, wq_ref, wqr_ref, gkvl_ref,
                wkn_ref, wvt_ref, gqh_ref, gqr_ref, gkh_ref, gkr_ref, gsv_ref, wsp_ref, bsp_ref,
                wpool_ref, pscale_ref, gosgu_ref, gopool_ref,
                q_out, k_out, v_out, mr_out, pext_ref):
    j = pl.program_id(1)
    h = _rms(x_ref[...], gmix_ref[...]).astype(BF16)
    z = jnp.dot(h, win_ref[...], preferred_element_type=F32)
    o_kv = Q_LORA
    o_kr = o_kv + KV_LORA
    o_u = o_kr + LANES
    o_v = o_u + SGU_WIDTH
    o_p = o_v + SGU_WIDTH

    c = c_ref[...]
    sn = sn_ref[...]
    ehead = ehead_ref[...]

    qn_lat = _rms(z[:, 0:Q_LORA], gql_ref[...]).astype(BF16)
    q = jnp.dot(qn_lat, wq_ref[...], preferred_element_type=F32)
    q_sw = jnp.dot(qn_lat, wqr_ref[...], preferred_element_type=F32)
    rq = lax.rsqrt(jnp.dot((q * q).astype(BF16), ehead, preferred_element_type=F32) * (1.0 / QK_HEAD) + EPS)

    kvn = _rms(z[:, o_kv:o_kr], gkvl_ref[...]).astype(BF16)
    kn = jnp.dot(kvn, wkn_ref[...], preferred_element_type=F32)
    vt = lax.dot_general(wvt_ref[...], kvn, (((1,), (1,)), ((), ())), preferred_element_type=F32)
    kr = z[:, o_kr:o_u]
    kr_sw = pltpu.roll(kr, LANES - ROPE_HALF, 1)
    kf = jnp.concatenate([kn[:, hh * HEAD_PAD:(hh + 1) * HEAD_PAD] + kr for hh in range(MLA_HEADS)], axis=1)
    rk = lax.rsqrt(jnp.dot((kf * kf).astype(BF16), ehead, preferred_element_type=F32) * (1.0 / QK_HEAD) + EPS)
    ones_rows = (lax.broadcasted_iota(jnp.int32, (V_AUG - V_HEAD, TKS), 0) == 0).astype(BF16)
    for hh in range(MLA_HEADS):
        sl = slice(hh * HEAD_PAD, (hh + 1) * HEAD_PAD)
        r = rq[:, sl]
        q_out[hh] = (((q[:, sl] * r * gqh_ref[...]) * c + (q_sw[:, sl] * r * gqr_ref[...]) * sn)
                     * Q_SCALE).astype(BF16)
        r = rk[:, sl]
        k_out[hh] = ((kf[:, sl] * r * gkh_ref[...]) * c + (kr_sw * r * gkr_ref[...]) * sn).astype(BF16)
        for t in range(TM // TKS):
            v_out[hh, t, 0:V_HEAD, :] = vt[hh * V_HEAD:(hh + 1) * V_HEAD, t * TKS:(t + 1) * TKS].astype(BF16)
            v_out[hh, t, V_HEAD:V_AUG, :] = ones_rows

    u = z[:, o_u:o_v]
    vs = _rms(z[:, o_v:o_p], gsv_ref[...])
    row = lax.broadcasted_iota(jnp.int32, (CHUNK, SGU_HEADS * CHUNK), 0)
    col = lax.broadcasted_iota(jnp.int32, (CHUNK, SGU_HEADS * CHUNK), 1)
    wt = jnp.where((col & (CHUNK - 1)) <= row, wsp_ref[...], 0.0).astype(BF16)
    lane_head = lax.broadcasted_iota(jnp.int32, (1, SGU_WIDTH), 1) // SGU_HEAD_DIM
    for cc in range(TM // CHUNK):
        r0, r1 = cc * CHUNK, (cc + 1) * CHUNK
        vc = vs[r0:r1, :]
        vbd = jnp.concatenate([jnp.where(lane_head == hh, vc, 0.0) for hh in range(SGU_HEADS)],
                              axis=0).astype(BF16)
        zc = jnp.dot(wt, vbd, preferred_element_type=F32) + bsp_ref[...]
        gm = u[r0:r1, :] * zc
        mr_out[r0:r1, 0:SGU_WIDTH] = _rms(gm, gosgu_ref[...]).astype(BF16)

    pin = z[:, o_p:o_p + POOL_WIDTH]

    @pl.when(j == 0)
    def _():
        pext_ref[0:MAX_WIN, :] = jnp.zeros((MAX_WIN, POOL_WIDTH), F32)

    @pl.when(j != 0)
    def _():
        pext_ref[0:MAX_WIN, :] = pext_ref[TM:TM + MAX_WIN, :]

    pext_ref[MAX_WIN:MAX_WIN + TM, :] = pin

    def shifted(d, lo, hi):
        return pext_ref[MAX_WIN - d:MAX_WIN - d + TM, lo:hi]

    lane = lax.broadcasted_iota(jnp.int32, (1, LANES), 1)
    a = [shifted(d, 0, LANES) for d in range(4)]
    s2 = a[0] + a[1]
    s4 = s2 + (a[2] + a[3])
    b = [shifted(d, LANES, 2 * LANES) for d in range(16)]
    s8 = ((b[0] + b[1]) + (b[2] + b[3])) + ((b[4] + b[5]) + (b[6] + b[7]))
    s16 = s8 + (((b[8] + b[9]) + (b[10] + b[11])) + ((b[12] + b[13]) + (b[14] + b[15])))
    first_group = lane < POOL_GROUP_DIM
    wsum = jnp.concatenate([jnp.where(first_group, s2, s4), jnp.where(first_group, s8, s16)], axis=1)
    lane_p = lax.broadcasted_iota(jnp.int32, (1, POOL_WIDTH), 1) // POOL_GROUP_DIM
    win = jnp.where(lane_p == 0, float(POOL_WINDOWS[0]),
                    jnp.where(lane_p == 1, float(POOL_WINDOWS[1]),
                              jnp.where(lane_p == 2, float(POOL_WINDOWS[2]), float(POOL_WINDOWS[3]))))
    t1 = (lax.broadcasted_iota(jnp.int32, (TM, POOL_WIDTH), 0) + (j * TM + 1)).astype(F32)
    count = jnp.minimum(t1, win)
    m = (wsum / count - pin).astype(BF16)
    y = jnp.dot(m, wpool_ref[...], preferred_element_type=F32) * pscale_ref[...]
    mr_out[:, SGU_WIDTH:SGU_WIDTH + POOL_WIDTH] = _rms(y, gopool_ref[...]).astype(BF16)


def _pre_mixer(x, ctab, stab, ehead, p, layer):
    B, S, _ = x.shape
    nt = S // TM
    names = ["g_mix", "w_in", "g_ql", "w_q", "w_qr", "g_kvl", "w_kn", "w_vt", "g_qh", "g_qr", "g_kh", "g_kr",
             "g_sv", "w_sp", "b_sp", "w_pool", "pscale", "g_osgu", "g_opool"]
    head_spec = pl.BlockSpec((None, MLA_HEADS, TM, HEAD_PAD), lambda b, j: (b, 0, j, 0))
    head_shape = jax.ShapeDtypeStruct((B, MLA_HEADS, S, HEAD_PAD), BF16)
    return pl.pallas_call(
        _pre_kernel,
        grid=(B, nt),
        in_specs=[pl.BlockSpec((None, TM, D_MODEL), lambda b, j: (b, j, 0)),
                  pl.BlockSpec((None, TM, LANES), lambda b, j: (b, j, 0)),
                  pl.BlockSpec((None, TM, LANES), lambda b, j: (b, j, 0)),
                  _const_spec(ehead.shape)]
                 + [_layer_spec(p[n].shape, layer) for n in names],
        out_specs=[head_spec, head_spec,
                   pl.BlockSpec((None, MLA_HEADS, TM // TKS, V_AUG, TKS), lambda b, j: (b, 0, j, 0, 0)),
                   pl.BlockSpec((None, TM, SGU_WIDTH + POOL_WIDTH), lambda b, j: (b, j, 0))],
        out_shape=[head_shape, head_shape,
                   jax.ShapeDtypeStruct((B, MLA_HEADS, S // TKS, V_AUG, TKS), BF16),
                   jax.ShapeDtypeStruct((B, S, SGU_WIDTH + POOL_WIDTH), BF16)],
        scratch_shapes=[pltpu.VMEM((TM + MAX_WIN, POOL_WIDTH), F32)],
        compiler_params=pltpu.CompilerParams(dimension_semantics=("arbitrary", "arbitrary"),
                                             vmem_limit_bytes=VMEM_LIMIT),
        name="pre_mixer",
    )(x, ctab, stab, ehead, *[p[n] for n in names])


def _attn_kernel(q_ref, k_ref, v_ref, bias_ref, g_ref, o_ref, m_scr, acc_scr, ring_scr):
    qi = pl.program_id(1)
    m_scr[...] = jnp.full(m_scr.shape, NEG_BIG, F32)
    acc_scr[...] = jnp.zeros(acc_scr.shape, F32)

    def scores(hh, tile, q_lo=0):
        kblk = k_ref[hh, pl.ds(pl.multiple_of(tile * TKS, TKS), TKS), :]
        return lax.dot_general(kblk, q_ref[hh, q_lo:, :], (((1,), (1,)), ((), ())),
                               preferred_element_type=F32)

    def softmax_update(hh, read_s, q_lo=0):
        m_prev = m_scr[hh, :, q_lo:]
        m_next = jnp.maximum(m_prev, jnp.max(read_s(), axis=0, keepdims=True))
        p = jnp.exp2(read_s() - m_next)
        alpha = jnp.exp2(m_prev - m_next)
        m_scr[hh, :, q_lo:] = m_next
        return p.astype(BF16), alpha

    def accumulate(hh, tile, p, alpha, q_lo=0):
        acc_scr[hh, :, q_lo:] = acc_scr[hh, :, q_lo:] * alpha + jnp.dot(
            v_ref[hh, tile], p, preferred_element_type=F32)

    def chain_at(base, i):
        return i % MLA_HEADS, base + i // MLA_HEADS

    def issue_scores(base, i, q_lo):
        ring_scr[i % RING, :, q_lo:] = scores(*chain_at(base, i), q_lo)

    def kv_group(base, n_sub, masked):
        q_lo = lambda i: (i // MLA_HEADS) * TKS if masked else 0
        for i in range(n_sub * MLA_HEADS):
            ahead = i + LOOKAHEAD
            if not masked or ahead < n_sub * MLA_HEADS:
                issue_scores(base, ahead, q_lo(ahead))
            hh, tile = chain_at(base, i)
            slot = ring_scr.at[i % RING, :, q_lo(i):]
            if masked:
                visible = bias_ref[i // MLA_HEADS, :, q_lo(i):] >= 0.0
                read_s = lambda: jnp.where(visible, slot[...], NEG_BIG)
            else:
                read_s = lambda: slot[...]
            p, alpha = softmax_update(hh, read_s, q_lo(i))
            accumulate(hh, tile, p, alpha, q_lo(i))

    def group_loop(n_groups, first_tile, n_sub):
        def body(g, c):
            kv_group(first_tile + g * n_sub, n_sub, False)
            return c
        lax.fori_loop(0, n_groups, body, 0)

    for i in range(LOOKAHEAD):
        issue_scores(0, i, 0)
    n_double = lax.shift_right_logical(qi, 1)
    group_loop(n_double, 0, 2 * DIAG_SUB)
    group_loop(qi & 1, n_double * (2 * DIAG_SUB), DIAG_SUB)
    kv_group(qi * DIAG_SUB, DIAG_SUB, True)
    a = jnp.concatenate([(acc_scr[hh, 0:V_HEAD, :] / acc_scr[hh, V_HEAD:V_HEAD + 1, :]).T
                         for hh in range(MLA_HEADS)], axis=1)
    o_ref[...] = _rms(a, g_ref[...]).astype(BF16)


def _attention(q, k, v, g_out, layer):
    B, H, S, _ = q.shape
    key = jnp.arange(TQ)[:, None]
    bias = jnp.where(key <= jnp.arange(TQ)[None, :], 0.0, NEG_BIG).astype(F32).reshape(DIAG_SUB, TKS, TQ)
    k_spec = pl.BlockSpec((None, H, S, HEAD_PAD), lambda b, i: (b, 0, 0, 0))
    v_spec = pl.BlockSpec((None, H, S // TKS, V_AUG, TKS), lambda b, i: (b, 0, 0, 0, 0))
    return pl.pallas_call(
        _attn_kernel,
        grid=(B, S // TQ),
        in_specs=[pl.BlockSpec((None, H, TQ, HEAD_PAD), lambda b, i: (b, 0, i, 0)),
                  k_spec, v_spec, _const_spec(bias.shape), _layer_spec(g_out.shape, layer)],
        out_specs=pl.BlockSpec((None, TQ, H * V_HEAD), lambda b, i: (b, i, 0)),
        out_shape=jax.ShapeDtypeStruct((B, S, H * V_HEAD), BF16),
        scratch_shapes=[pltpu.VMEM((H, 1, TQ), F32), pltpu.VMEM((H, V_AUG, TQ), F32),
                        pltpu.VMEM((RING, TKS, TQ), F32)],
        compiler_params=pltpu.CompilerParams(dimension_semantics=("parallel", "arbitrary"),
                                             vmem_limit_bytes=VMEM_LIMIT),
        name="mla_attention",
    )(q, k, v, bias, g_out)


def _post_kernel(x_ref, a_ref, mr_ref, wo_ref, gffn_ref, wg_ref, wu_ref, wd_ref, o_ref, h2_scr):
    n_a = a_ref.shape[1]
    x1 = (x_ref[...]
          + jnp.dot(a_ref[...], wo_ref[0:n_a, :], preferred_element_type=F32)
          + jnp.dot(mr_ref[...], wo_ref[n_a:, :], preferred_element_type=F32))
    h2_scr[...] = _rms(x1, gffn_ref[...]).astype(BF16)
    o_ref[...] = x1
    for cidx in range(N_FFN_CHUNKS):
        lo, hi = cidx * FFN_CHUNK, (cidx + 1) * FFN_CHUNK
        h2 = h2_scr[...]
        gate = jnp.dot(h2, wg_ref[:, lo:hi], preferred_element_type=F32)
        up = jnp.dot(h2, wu_ref[:, lo:hi], preferred_element_type=F32)
        act = (gate * (1.0 / (1.0 + jnp.exp(-gate))) * up).astype(BF16)
        o_ref[...] += jnp.dot(act, wd_ref[lo:hi, :], preferred_element_type=F32)


def _post_mixer(x2d, a2d, mr2d, p, layer):
    n = x2d.shape[0]
    names = ["w_out", "g_ffn", "w_gate", "w_up", "w_down"]
    return pl.pallas_call(
        _post_kernel,
        grid=(n // TM,),
        in_specs=[pl.BlockSpec((TM, D_MODEL), lambda i: (i, 0)),
                  pl.BlockSpec((TM, a2d.shape[1]), lambda i: (i, 0)),
                  pl.BlockSpec((TM, mr2d.shape[1]), lambda i: (i, 0))]
                 + [_layer_spec(p[nm].shape, layer, pipeline_mode=pl.Buffered(1)) for nm in names],
        out_specs=pl.BlockSpec((TM, D_MODEL), lambda i: (i, 0)),
        out_shape=jax.ShapeDtypeStruct((n, D_MODEL), F32),
        scratch_shapes=[pltpu.VMEM((TM, D_MODEL), BF16)],
        compiler_params=pltpu.CompilerParams(dimension_semantics=("parallel",),
                                             vmem_limit_bytes=VMEM_LIMIT),
        name="post_mixer",
    )(x2d, a2d, mr2d, *[p[nm] for nm in names])


def _prep_params(g_mix_norm, w_in, g_q_lat, w_q_up, g_kv_lat, w_kv_up, g_q_head, g_k_head,
                 g_sgu_v, w_spatial, b_spatial, w_pool, pool_scale, g_out_mla, g_out_sgu,
                 g_out_pool, w_out, g_ffn_norm, w_gate, w_up, w_down):
    depth = w_in.shape[0]
    o2 = Q_LORA + KV_LORA
    o3 = o2 + QK_ROPE
    zeros = functools.partial(jnp.zeros, dtype=F32)
    row = lambda g: g[:, None, :]

    kr = w_in[:, :, o2:o3]
    kr_blk = jnp.concatenate([zeros((depth, D_MODEL, QK_NOPE)), kr, kr[:, :, :ROPE_HALF],
                              zeros((depth, D_MODEL, LANES - ROPE_COPY_END))], axis=2)
    w_in_p = jnp.concatenate([w_in[:, :, :o2], kr_blk, w_in[:, :, o3:]], axis=2).astype(BF16)

    def swap_halves(t):
        lead = t.shape[:-1]
        return jnp.concatenate([zeros(lead + (QK_NOPE,)), t[..., QK_NOPE + ROPE_HALF:QK_HEAD],
                                t[..., QK_NOPE:QK_NOPE + ROPE_HALF], zeros(lead + (LANES - QK_HEAD,))], axis=-1)

    wq = w_q_up.reshape(depth, Q_LORA, MLA_HEADS, QK_HEAD)
    wq_p = jnp.concatenate([wq, zeros((depth, Q_LORA, MLA_HEADS, LANES - QK_HEAD))], axis=3)
    wq_p = wq_p.reshape(depth, Q_LORA, QK_WIDTH).astype(BF16)
    wqr_p = swap_halves(wq).reshape(depth, Q_LORA, QK_WIDTH).astype(BF16)

    wkv = w_kv_up.reshape(depth, KV_LORA, MLA_HEADS, QK_NOPE + V_HEAD)
    wkn = jnp.concatenate([wkv[..., :QK_NOPE], zeros((depth, KV_LORA, MLA_HEADS, HEAD_PAD - QK_NOPE))], axis=3)
    wkn_p = wkn.reshape(depth, KV_LORA, QK_WIDTH).astype(BF16)
    wvt_p = jnp.swapaxes(wkv[..., QK_NOPE:].reshape(depth, KV_LORA, MLA_HEADS * V_HEAD), 1, 2).astype(BF16)

    def pad_gain(g, copy):
        tail = g[:, QK_NOPE:QK_NOPE + ROPE_HALF] if copy else zeros((depth, ROPE_HALF))
        return row(jnp.concatenate([g, tail, zeros((depth, LANES - ROPE_COPY_END))], axis=1))

    w_sp = jnp.transpose(w_spatial, (0, 2, 1, 3)).reshape(depth, CHUNK, SGU_HEADS * CHUNK)
    b_sp = jnp.repeat(jnp.swapaxes(b_spatial, 1, 2), SGU_HEAD_DIM, axis=2)
    n_g = len(POOL_WINDOWS)
    eye = jnp.eye(n_g, dtype=F32)
    w_pool_bd = (eye[None, :, None, :, None] * w_pool[:, :, :, None, :]).reshape(
        depth, n_g * POOL_GROUP_DIM, n_g * POOL_GROUP_DIM).astype(BF16)

    return dict(
        g_mix=row(g_mix_norm), w_in=w_in_p, g_ql=row(g_q_lat), w_q=wq_p, w_qr=wqr_p,
        g_kvl=row(g_kv_lat), w_kn=wkn_p, w_vt=wvt_p,
        g_qh=pad_gain(g_q_head, False), g_qr=row(swap_halves(g_q_head)),
        g_kh=pad_gain(g_k_head, True), g_kr=row(swap_halves(g_k_head)),
        g_sv=row(g_sgu_v), w_sp=w_sp, b_sp=b_sp, w_pool=w_pool_bd,
        pscale=row(pool_scale), g_osgu=row(g_out_sgu), g_opool=row(g_out_pool), g_omla=row(g_out_mla),
        w_out=w_out.astype(BF16), g_ffn=row(g_ffn_norm),
        w_gate=w_gate.astype(BF16), w_up=w_up.astype(BF16), w_down=w_down.astype(BF16),
    )


def kernel(x, positions, g_mix_norm, w_in, g_q_lat, w_q_up, g_kv_lat, w_kv_up, g_q_head, g_k_head, g_sgu_v, w_spatial, b_spatial, w_pool, pool_scale, g_out_mla, g_out_sgu, g_out_pool, w_out, g_ffn_norm, w_gate, w_up, w_down):
    B, S, D = x.shape
    depth = w_in.shape[0]
    assert D == D_MODEL and S % TM == 0 and S % TQ == 0 and TM % TKS == 0 and TM % CHUNK == 0

    inv_freq = 1.0 / (ROPE_THETA ** (jnp.arange(ROPE_HALF, dtype=F32) / ROPE_HALF))
    ctab, stab = _rope_tables(positions.astype(F32).reshape(B * S // TM, 1, TM), inv_freq[:, None])
    ctab = ctab.reshape(B, S, LANES)
    stab = stab.reshape(B, S, LANES)

    lane = jnp.arange(QK_WIDTH)
    ehead = ((lane[:, None] // HEAD_PAD == lane[None, :] // HEAD_PAD)
             & (lane[:, None] % HEAD_PAD < QK_HEAD)).astype(BF16)

    p = _prep_params(g_mix_norm, w_in, g_q_lat, w_q_up, g_kv_lat, w_kv_up, g_q_head, g_k_head,
                     g_sgu_v, w_spatial, b_spatial, w_pool, pool_scale, g_out_mla, g_out_sgu,
                     g_out_pool, w_out, g_ffn_norm, w_gate, w_up, w_down)
    for l in range(depth):
        q, k, v, mr = _pre_mixer(x, ctab, stab, ehead, p, l)
        a = _attention(q, k, v, p["g_omla"], l)
        x = _post_mixer(x.reshape(B * S, D), a.reshape(B * S, -1), mr.reshape(B * S, -1), p, l).reshape(B, S, D)
    return x
```
